```python
import jax, jax.numpy as jnp
from jax import lax
import numpy as np

D_MODEL = 2048
BATCH = 4
SEQ = 2048
DEPTH = 1

MIX_W = D_MODEL
POOL_W = MIX_W // 2
POOL_WINDOWS = (2, 4, 8, 16)
POOL_GROUPS = len(POOL_WINDOWS)
POOL_CH = POOL_W // POOL_GROUPS
ATTN_W = MIX_W - POOL_W
HEAD_DIM = 128
N_HEADS = ATTN_W // HEAD_DIM
KV_HEADS = 2
Q_PER_KV = N_HEADS // KV_HEADS
KV_W = KV_HEADS * HEAD_DIM
IDX_HEADS = 16
IDX_DIM = 128
INDEX_TOPK = 256
Q_BLOCK = 128
D_FF = 4 * D_MODEL
RMS_EPS = 1e-6
IN_SIZES = (POOL_W, ATTN_W, KV_W, KV_W, IDX_HEADS * IDX_DIM, IDX_DIM, IDX_HEADS)
IN_W = sum(IN_SIZES)
IN_SPLITS = tuple(int(v) for v in np.cumsum(IN_SIZES)[:-1])

kernel_name = "hymba_pool_dsa_hybrid_layer"


def rms_norm(x, g):
    xf = x.astype(jnp.float32)
    y = xf * lax.rsqrt(jnp.mean(xf * xf, axis=-1, keepdims=True) + RMS_EPS)
    return (y * g.astype(jnp.float32)).astype(x.dtype)


def modulate(h, shift, scale):
    return h * (1 + scale[:, None, :]) + shift[:, None, :]


def multiscale_pool_mixer(u, w_pool, pool_scale):
    B, S, _ = u.shape
    ug = u.reshape(B, S, POOL_GROUPS, POOL_CH)
    cs = jnp.cumsum(ug.astype(jnp.float32), axis=1)
    cs = jnp.pad(cs, ((0, 0), (1, 0), (0, 0), (0, 0)))
    t = jnp.arange(S)
    outs = []
    for gi, win in enumerate(POOL_WINDOWS):
        lo = jnp.maximum(t + 1 - win, 0)
        total = cs[:, 1:, gi] - cs[:, lo, gi]
        count = jnp.minimum(t + 1, win).astype(jnp.float32)
        outs.append(total / count[None, :, None] - ug[:, :, gi].astype(jnp.float32))
    pooled = jnp.stack(outs, axis=2).astype(u.dtype)
    mixed = jnp.einsum('bsgc,gcd->bsgd', pooled, w_pool)
    return mixed.reshape(B, S, POOL_W) * pool_scale


def dsa_sparse_attention(q, k, v, q_idx, k_idx, w_idx):
    B, S = q.shape[0], q.shape[1]
    topk = min(INDEX_TOPK, S // 4)
    nb = S // Q_BLOCK
    idx_scale = (IDX_HEADS ** -0.5) * (IDX_DIM ** -0.5)
    key_pos = jnp.arange(S)
    gather = jax.vmap(lambda a, i: a[i])

    def to_blocks(a):
        return a.reshape((B, nb, Q_BLOCK) + a.shape[2:]).swapaxes(0, 1)

    def one_block(args):
        qb, qib, wb, t0 = args
        tpos = t0 + jnp.arange(Q_BLOCK)
        rel = jax.nn.relu(jnp.einsum('bthd,bsd->bths', qib, k_idx).astype(jnp.float32))
        score = jnp.einsum('bth,bths->bts', wb.astype(jnp.float32), rel) * idx_scale
        causal = key_pos[None, :] <= tpos[:, None]
        score = jnp.where(causal[None], score, -jnp.inf)
        _, sel = lax.top_k(score, topk)
        k_sel = gather(k, sel)
        v_sel = gather(v, sel)
        qg = qb.reshape(B, Q_BLOCK, KV_HEADS, Q_PER_KV, HEAD_DIM)
        logits = jnp.einsum('btgrd,btkgd->btgrk', qg, k_sel).astype(jnp.float32) * (HEAD_DIM ** -0.5)
        valid = (sel <= tpos[None, :, None])[:, :, None, None, :]
        logits = jnp.where(valid, logits, -jnp.inf)
        p = jax.nn.softmax(logits, axis=-1).astype(v.dtype)
        o = jnp.einsum('btgrk,btkgd->btgrd', p, v_sel)
        return o.reshape(B, Q_BLOCK, ATTN_W)

    starts = jnp.arange(nb) * Q_BLOCK
    out = lax.map(one_block, (to_blocks(q), to_blocks(q_idx), to_blocks(w_idx), starts))
    return out.swapaxes(0, 1).reshape(B, S, ATTN_W)


def setup_inputs(seed: int = 0) -> dict:
    key = jax.random.key(seed)
    ks = jax.random.split(key, 20)
    f32 = jnp.float32
    D = D_MODEL

    def nrm(k, shape, scale):
        return jax.random.normal(k, shape, f32) * scale

    def gain(k, shape):
        return 1.0 + 0.02 * jax.random.normal(k, shape, f32)

    return {
        "x": jax.random.normal(ks[0], (BATCH, SEQ, D), f32),
        "c": jax.random.normal(ks[1], (BATCH, D), f32),
        "w_ada": nrm(ks[2], (DEPTH, D, 6 * D), D ** -0.5),
        "b_ada": nrm(ks[3], (DEPTH, 6 * D), 0.01),
        "g_pre_mix": gain(ks[4], (DEPTH, D)),
        "g_post_mix": gain(ks[5], (DEPTH, D)),
        "g_pre_ffn": gain(ks[6], (DEPTH, D)),
        "g_post_ffn": gain(ks[7], (DEPTH, D)),
        "w_in": nrm(ks[8], (DEPTH, D, IN_W), D ** -0.5),
        "w_pool": nrm(ks[9], (DEPTH, POOL_GROUPS, POOL_CH, POOL_CH), POOL_CH ** -0.5),
        "pool_scale": gain(ks[10], (DEPTH, POOL_W)),
        "g_pool_out": gain(ks[11], (DEPTH, POOL_W)),
        "g_attn_out": gain(ks[12], (DEPTH, ATTN_W)),
        "w_out": nrm(ks[13], (DEPTH, MIX_W, D), MIX_W ** -0.5),
        "w_ff1": nrm(ks[14], (DEPTH, D, D_FF), D ** -0.5),
        "w_ff2": nrm(ks[15], (DEPTH, D_FF, D), D_FF ** -0.5),
    }


def reference(x, c, w_ada, b_ada, g_pre_mix, g_post_mix, g_pre_ffn, g_post_ffn,
              w_in, w_pool, pool_scale, g_pool_out, g_attn_out, w_out, w_ff1, w_ff2):
    B, S, D = x.shape
    c_act = jax.nn.silu(c)
    for l in range(DEPTH):
        mod = jnp.einsum('bd,de->be', c_act, w_ada[l]) + b_ada[l]
        shift1, scale1, gate1, shift2, scale2, gate2 = jnp.split(mod, 6, axis=-1)

        h = modulate(rms_norm(x, g_pre_mix[l]), shift1, scale1)
        z = jnp.einsum('bsd,de->bse', h, w_in[l])
        z_pool, z_q, z_k, z_v, z_iq, z_ik, z_iw = jnp.split(z, IN_SPLITS, axis=-1)
        pool_out = multiscale_pool_mixer(z_pool, w_pool[l], pool_scale[l])
        attn_out = dsa_sparse_attention(
            z_q.reshape(B, S, N_HEADS, HEAD_DIM),
            z_k.reshape(B, S, KV_HEADS, HEAD_DIM),
            z_v.reshape(B, S, KV_HEADS, HEAD_DIM),
            z_iq.reshape(B, S, IDX_HEADS, IDX_DIM),
            z_ik, z_iw)
        merged = jnp.concatenate(
            [rms_norm(pool_out, g_pool_out[l]), rms_norm(attn_out, g_attn_out[l])], axis=-1)
        mix = jnp.einsum('bse,ed->bsd', merged, w_out[l])
        x = x + gate1[:, None, :] * rms_norm(mix, g_post_mix[l])

        h2 = modulate(rms_norm(x, g_pre_ffn[l]), shift2, scale2)
        f = jnp.square(jax.nn.relu(jnp.einsum('bsd,df->bsf', h2, w_ff1[l])))
        f = jnp.einsum('bsf,fd->bsd', f, w_ff2[l])
        x = x + gate2[:, None, :] * rms_norm(f, g_post_ffn[l])
    return x
```

```python
import functools

import jax
import jax.numpy as jnp
from jax import lax
from jax.experimental import pallas as pl
from jax.experimental.pallas import tpu as pltpu

POOL_WINDOWS = (2, 4, 8, 16)
POOL_GROUPS = len(POOL_WINDOWS)
HEAD_DIM = 128
KV_HEADS = 2
IDX_HEADS = 16
IDX_DIM = 128
INDEX_TOPK = 256
RMS_EPS = 1e-6

V7X_LANES = 128
V7X_VMEM_LIMIT_BYTES = 56 * 1024 * 1024

BF16 = jnp.bfloat16
F32 = jnp.float32
MAX_BISECT_STEPS = 320


def _rms(xf, g):
    return xf * lax.rsqrt(jnp.mean(xf * xf, axis=-1, keepdims=True) + RMS_EPS) * g


def _params(sem):
    return pltpu.CompilerParams(dimension_semantics=sem, vmem_limit_bytes=V7X_VMEM_LIMIT_BYTES)


def _ada_kernel(c_ref, w_ref, b_ref, o_ref):
    c = c_ref[...]
    c_act = (c * jax.nn.sigmoid(c)).astype(BF16)
    o_ref[...] = jnp.dot(c_act, w_ref[...].astype(BF16), preferred_element_type=F32) + b_ref[...]


def _ada_mod(c_pad, w_ada, b_ada, *, tn):
    mp, d = c_pad.shape
    n = w_ada.shape[1]
    return pl.pallas_call(
        _ada_kernel,
        out_shape=jax.ShapeDtypeStruct((mp, n), F32),
        grid=(n // tn,),
        in_specs=[pl.BlockSpec((mp, d), lambda j: (0, 0)),
                  pl.BlockSpec((d, tn), lambda j: (0, j)),
                  pl.BlockSpec((1, tn), lambda j: (0, j))],
        out_specs=pl.BlockSpec((mp, tn), lambda j: (0, j)),
        compiler_params=_params(("arbitrary",)),
        name="ada_mod",
    )(c_pad, w_ada, b_ada)


def _in_proj_kernel(x_ref, g_ref, sc_ref, sh_ref, w_ref,
                    zp_ref, q_ref, k_ref, v_ref, iq_ref, ik_ref, iw_ref, *, splits):
    pool_w, attn_w, kv_w, iq_w = splits
    h = (_rms(x_ref[...], g_ref[...]) * (1.0 + sc_ref[0]) + sh_ref[0]).astype(BF16)

    def proj(lo, width):
        return jnp.dot(h, w_ref[:, lo:lo + width], preferred_element_type=F32)

    off = 0
    zp_ref[...] = proj(off, pool_w).astype(BF16)
    off += pool_w
    for hd in range(attn_w // HEAD_DIM):
        q_ref[hd] = proj(off + hd * HEAD_DIM, HEAD_DIM).astype(BF16)
    off += attn_w
    k_ref[...] = proj(off, kv_w).astype(BF16)
    off += kv_w
    v_ref[...] = proj(off, kv_w).astype(BF16)
    off += kv_w
    for hd in range(iq_w // IDX_DIM):
        iq_ref[hd] = proj(off + hd * IDX_DIM, IDX_DIM).astype(BF16)
    off += iq_w
    ik_ref[...] = proj(off, IDX_DIM).astype(BF16)
    off += IDX_DIM
    iw_ref[...] = proj(off, V7X_LANES)[:, :IDX_HEADS]


def _in_proj(x2, g, scale, shift, w_in_p, *, tm, seq, splits):
    m, d = x2.shape
    pool_w, attn_w, kv_w, iq_w = splits
    n_heads = attn_w // HEAD_DIM
    per_b = seq // tm
    row = lambda i: (i, 0)
    hrow = lambda i: (0, i, 0)
    return pl.pallas_call(
        functools.partial(_in_proj_kernel, splits=splits),
        out_shape=(jax.ShapeDtypeStruct((m, pool_w), BF16),
                   jax.ShapeDtypeStruct((n_heads, m, HEAD_DIM), BF16),
                   jax.ShapeDtypeStruct((m, kv_w), BF16),
                   jax.ShapeDtypeStruct((m, kv_w), BF16),
                   jax.ShapeDtypeStruct((IDX_HEADS, m, IDX_DIM), BF16),
                   jax.ShapeDtypeStruct((m, IDX_DIM), BF16),
                   jax.ShapeDtypeStruct((m, IDX_HEADS), F32)),
        grid=(m // tm,),
        in_specs=[pl.BlockSpec((tm, d), row),
                  pl.BlockSpec((1, d), lambda i: (0, 0)),
                  pl.BlockSpec((1, 1, d), lambda i: (i // per_b, 0, 0)),
                  pl.BlockSpec((1, 1, d), lambda i: (i // per_b, 0, 0)),
                  pl.BlockSpec(w_in_p.shape, lambda i: (0, 0), pipeline_mode=pl.Buffered(1))],
        out_specs=(pl.BlockSpec((tm, pool_w), row),
                   pl.BlockSpec((n_heads, tm, HEAD_DIM), hrow),
                   pl.BlockSpec((tm, kv_w), row),
                   pl.BlockSpec((tm, kv_w), row),
                   pl.BlockSpec((IDX_HEADS, tm, IDX_DIM), hrow),
                   pl.BlockSpec((tm, IDX_DIM), row),
                   pl.BlockSpec((tm, IDX_HEADS), row)),
        compiler_params=_params(("arbitrary",)),
        name="in_proj",
    )(x2, g, scale, shift, w_in_p)


def _pool_kernel(cur_ref, prev_ref, w_ref, ps_ref, g_ref, o_ref, ext_ref, mix_ref, *, halo):
    t = pl.program_id(1)
    tt = cur_ref.shape[0]
    ch = w_ref.shape[1]
    prev = prev_ref[...].astype(F32)
    ext_ref[0:halo, :] = jnp.where(t > 0, prev, jnp.zeros_like(prev))
    ext_ref[halo:halo + tt, :] = cur_ref[...].astype(F32)
    pos = t * tt + lax.broadcasted_iota(jnp.int32, (tt, 1), 0)
    for gi, win in enumerate(POOL_WINDOWS):
        cols = slice(gi * ch, (gi + 1) * ch)
        total = ext_ref[halo:halo + tt, cols]
        for j in range(1, win):
            total = total + ext_ref[halo - j:halo - j + tt, cols]
        inv_cnt = 1.0 / jnp.minimum(pos + 1, win).astype(F32)
        pooled = total * inv_cnt - ext_ref[halo:halo + tt, cols]
        mixed = jnp.dot(pooled.astype(BF16), w_ref[gi], preferred_element_type=F32)
        mix_ref[:, cols] = mixed * ps_ref[:, cols]
    o_ref[...] = _rms(mix_ref[...], g_ref[...]).astype(BF16)


def _pool_mixer(z_pool, w_pool, pool_scale, g_pool, *, batch, seq, tt):
    m, pw = z_pool.shape
    halo = max(POOL_WINDOWS)
    per_b = seq // tt
    hb = tt // halo
    return pl.pallas_call(
        functools.partial(_pool_kernel, halo=halo),
        out_shape=jax.ShapeDtypeStruct((m, pw), BF16),
        grid=(batch, per_b),
        in_specs=[pl.BlockSpec((tt, pw), lambda b, t: (b * per_b + t, 0)),
                  pl.BlockSpec((halo, pw), lambda b, t: (jnp.maximum((b * per_b + t) * hb - 1, 0), 0)),
                  pl.BlockSpec(w_pool.shape, lambda b, t: (0, 0, 0)),
                  pl.BlockSpec((1, pw), lambda b, t: (0, 0)),
                  pl.BlockSpec((1, pw), lambda b, t: (0, 0))],
        out_specs=pl.BlockSpec((tt, pw), lambda b, t: (b * per_b + t, 0)),
        scratch_shapes=[pltpu.VMEM((tt + halo, pw), F32), pltpu.VMEM((tt, pw), F32)],
        compiler_params=_params(("arbitrary", "arbitrary")),
        name="pool_mixer",
    )(z_pool, z_pool, w_pool, pool_scale, g_pool)


def _dsa_kernel(q_ref, iq_ref, iw_ref, k_ref, v_ref, ik_ref, g_ref, o_ref,
                sc_ref, bias_ref, oacc_ref, *, tq, tk, topk, n_heads):
    qi = pl.program_id(1)
    t0 = qi * tq
    nkv = lax.div(t0 + tq + tk - 1, tk)
    q_per_kv = n_heads // KV_HEADS
    idx_scale = (IDX_HEADS ** -0.5) * (IDX_DIM ** -0.5)
    nt_dims = (((1,), (1,)), ((), ()))
    row_pos = t0 + lax.broadcasted_iota(jnp.int32, (tq, tk), 0)
    col_iota = lax.broadcasted_iota(jnp.int32, (tq, tk), 1)
    w_idx = iw_ref[...] * idx_scale

    def score_block(kb, carry):
        lo, hi = carry
        ik = ik_ref[pl.ds(pl.multiple_of(kb * tk, tk), tk), :]
        acc = jnp.zeros((tq, tk), F32)
        for h in range(IDX_HEADS):
            y = lax.dot_general(iq_ref[h], ik, nt_dims, preferred_element_type=F32)
            acc = acc + w_idx[:, h:h + 1] * jnp.maximum(y, 0.0)
        causal = (kb * tk + col_iota) <= row_pos
        sc_ref[kb] = jnp.where(causal, acc, -jnp.inf)
        lo = jnp.minimum(lo, jnp.min(jnp.where(causal, acc, jnp.inf), axis=-1, keepdims=True))
        hi = jnp.maximum(hi, jnp.max(jnp.where(causal, acc, -jnp.inf), axis=-1, keepdims=True))
        return lo, hi

    lo0, hi0 = lax.fori_loop(0, nkv, score_block,
                             (jnp.full((tq, 1), jnp.inf, F32), jnp.full((tq, 1), -jnp.inf, F32)))

    def count_ge(cand):
        def body(kb, cnt):
            return cnt + jnp.sum(jnp.where(sc_ref[kb] >= cand, 1.0, 0.0), axis=-1, keepdims=True)
        return lax.fori_loop(0, nkv, body, jnp.zeros((tq, 1), F32))

    few_keys = (t0 + lax.broadcasted_iota(jnp.int32, (tq, 1), 0)) < topk

    def bisect_cond(state):
        it, _, _, done = state
        return (it < MAX_BISECT_STEPS) & (jnp.min(done) < 0.5)

    def bisect_step(state):
        it, lo, hi, done = state
        mid = 0.5 * lo + 0.5 * hi
        cnt = count_ge(mid)
        ge = cnt >= float(topk)
        move = (done < 0.5) & (mid > lo) & (mid < hi)
        new_lo = jnp.where(move & ge, mid, lo)
        new_hi = jnp.where(move & ~ge, mid, hi)
        new_done = jnp.where(move & (cnt != float(topk)), 0.0, 1.0)
        return it + 1, new_lo, new_hi, new_done

    _, lo_fin, _, _ = lax.while_loop(
        bisect_cond, bisect_step,
        (jnp.int32(0), lo0, hi0, jnp.where(few_keys, 1.0, 0.0)))
    thr = jnp.where(few_keys, -jnp.inf, lo_fin)

    def bias_block(kb, carry):
        keep = (sc_ref[kb] >= thr) & ((kb * tk + col_iota) <= row_pos)
        bias_ref[kb] = jnp.where(keep, 0.0, -jnp.inf)
        return carry

    lax.fori_loop(0, nkv, bias_block, 0)

    c_exp2 = (HEAD_DIM ** -0.5) * 1.4426950408889634
    for hd in range(n_heads):
        g = hd // q_per_kv
        qh = q_ref[hd]
        cols = slice(g * HEAD_DIM, (g + 1) * HEAD_DIM)

        def attn_block(kb, carry, qh=qh, cols=cols):
            m_run, l_run, acc = carry
            rows = pl.ds(pl.multiple_of(kb * tk, tk), tk)
            kblk = k_ref[rows, cols]
            vblk = v_ref[rows, cols]
            s = lax.dot_general(qh, kblk, nt_dims, preferred_element_type=F32) * c_exp2 + bias_ref[kb]
            m_new = jnp.maximum(m_run, jnp.max(s, axis=-1, keepdims=True))
            m_safe = jnp.where(m_new == -jnp.inf, 0.0, m_new)
            alpha = jnp.exp2(m_run - m_safe)
            p = jnp.exp2(s - m_safe)
            l_new = alpha * l_run + jnp.sum(p, axis=-1, keepdims=True)
            acc_new = alpha * acc + jnp.dot(p.astype(BF16), vblk, preferred_element_type=F32)
            return m_new, l_new, acc_new

        init = (jnp.full((tq, 1), -jnp.inf, F32), jnp.zeros((tq, 1), F32), jnp.zeros((tq, HEAD_DIM), F32))
        _, l_fin, acc_fin = lax.fori_loop(0, nkv, attn_block, init)
        oacc_ref[:, hd * HEAD_DIM:(hd + 1) * HEAD_DIM] = acc_fin / l_fin

    o_ref[...] = _rms(oacc_ref[...], g_ref[...]).astype(BF16)


def _dsa_attention(q, iq, iw, k, v, ik, g_attn, *, batch, seq, tq, tk):
    n_heads, m, _ = q.shape
    attn_w = n_heads * HEAD_DIM
    kv_w = k.shape[1]
    per_b = seq // tq
    topk = min(INDEX_TOPK, seq // 4)
    qmap = lambda b, i: (0, b * per_b + i, 0)
    rmap = lambda b, i: (b * per_b + i, 0)
    bmap = lambda b, i: (b, 0)
    return pl.pallas_call(
        functools.partial(_dsa_kernel, tq=tq, tk=tk, topk=topk, n_heads=n_heads),
        out_shape=jax.ShapeDtypeStruct((m, attn_w), BF16),
        grid=(batch, per_b),
        in_specs=[pl.BlockSpec((n_heads, tq, HEAD_DIM), qmap),
                  pl.BlockSpec((IDX_HEADS, tq, IDX_DIM), qmap),
                  pl.BlockSpec((tq, IDX_HEADS), rmap),
                  pl.BlockSpec((seq, kv_w), bmap),
                  pl.BlockSpec((seq, kv_w), bmap),
                  pl.BlockSpec((seq, IDX_DIM), bmap),
                  pl.BlockSpec((1, attn_w), lambda b, i: (0, 0))],
        out_specs=pl.BlockSpec((tq, attn_w), rmap),
        scratch_shapes=[pltpu.VMEM((seq // tk, tq, tk), F32),
                        pltpu.VMEM((seq // tk, tq, tk), F32),
                        pltpu.VMEM((tq, attn_w), F32)],
        compiler_params=_params(("arbitrary", "arbitrary")),
        name="dsa_attn",
    )(q, iq, iw, k, v, ik, g_attn)


def _out_proj_kernel(pool_ref, attn_ref, w_ref, x_ref, gpost_ref, gate_ref, gpre_ref, sc_ref, sh_ref,
                     x1_ref, h2_ref):
    pw = pool_ref.shape[1]
    mix = jnp.dot(pool_ref[...], w_ref[0:pw, :], preferred_element_type=F32)
    mix = mix + jnp.dot(attn_ref[...], w_ref[pw:, :], preferred_element_type=F32)
    x1 = x_ref[...] + gate_ref[0] * _rms(mix, gpost_ref[...])
    x1_ref[...] = x1
    h2_ref[...] = (_rms(x1, gpre_ref[...]) * (1.0 + sc_ref[0]) + sh_ref[0]).astype(BF16)


def _out_proj(pool_n, attn_n, w_out, x2, g_post, gate, g_pre, scale, shift, *, tm, seq):
    m, d = x2.shape
    pw, aw = pool_n.shape[1], attn_n.shape[1]
    per_b = seq // tm
    row = lambda i: (i, 0)
    vec = lambda i: (0, 0)
    bvec = lambda i: (i // per_b, 0, 0)
    return pl.pallas_call(
        _out_proj_kernel,
        out_shape=(jax.ShapeDtypeStruct((m, d), F32), jax.ShapeDtypeStruct((m, d), BF16)),
        grid=(m // tm,),
        in_specs=[pl.BlockSpec((tm, pw), row),
                  pl.BlockSpec((tm, aw), row),
                  pl.BlockSpec(w_out.shape, vec, pipeline_mode=pl.Buffered(1)),
                  pl.BlockSpec((tm, d), row),
                  pl.BlockSpec((1, d), vec),
                  pl.BlockSpec((1, 1, d), bvec),
                  pl.BlockSpec((1, d), vec),
                  pl.BlockSpec((1, 1, d), bvec),
                  pl.BlockSpec((1, 1, d), bvec)],
        out_specs=(pl.BlockSpec((tm, d), row), pl.BlockSpec((tm, d), row)),
        compiler_params=_params(("arbitrary",)),
        name="out_proj",
    )(pool_n, attn_n, w_out, x2, g_post, gate, g_pre, scale, shift)


def _ffn_kernel(h_ref, w1_ref, w2_ref, x_ref, g_ref, gate_ref, o_ref, acc_ref):
    j = pl.program_id(1)

    @pl.when(j == 0)
    def _():
        acc_ref[...] = jnp.zeros_like(acc_ref)

    a = jnp.dot(h_ref[...], w1_ref[...], preferred_element_type=F32)
    a = jnp.square(jnp.maximum(a, 0.0)).astype(BF16)
    acc_ref[...] += jnp.dot(a, w2_ref[...], preferred_element_type=F32)

    @pl.when(j == pl.num_programs(1) - 1)
    def _():
        o_ref[...] = x_ref[...] + gate_ref[0] * _rms(acc_ref[...], g_ref[...])


def _ffn(h2, w1, w2, x1, g_post, gate, *, tm, tf, seq):
    m, d = x1.shape
    f = w1.shape[1]
    per_b = seq // tm
    return pl.pallas_call(
        _ffn_kernel,
        out_shape=jax.ShapeDtypeStruct((m, d), F32),
        grid=(m // tm, f // tf),
        in_specs=[pl.BlockSpec((tm, d), lambda i, j: (i, 0)),
                  pl.BlockSpec((d, tf), lambda i, j: (0, j)),
                  pl.BlockSpec((tf, d), lambda i, j: (j, 0)),
                  pl.BlockSpec((tm, d), lambda i, j: (i, 0)),
                  pl.BlockSpec((1, d), lambda i, j: (0, 0)),
                  pl.BlockSpec((1, 1, d), lambda i, j: (i // per_b, 0, 0))],
        out_specs=pl.BlockSpec((tm, d), lambda i, j: (i, 0)),
        scratch_shapes=[pltpu.VMEM((tm, d), F32)],
        compiler_params=_params(("arbitrary", "arbitrary")),
        name="ffn",
    )(h2, w1, w2, x1, g_post, gate)


def kernel(x, c, w_ada, b_ada, g_pre_mix, g_post_mix, g_pre_ffn, g_post_ffn, w_in, w_pool, pool_scale,
           g_pool_out, g_attn_out, w_out, w_ff1, w_ff2):
    batch, seq, d = x.shape
    depth = w_ada.shape[0]
    m = batch * seq
    pool_w = pool_scale.shape[1]
    attn_w = g_attn_out.shape[1]
    kv_w = KV_HEADS * HEAD_DIM
    iq_w = IDX_HEADS * IDX_DIM
    splits = (pool_w, attn_w, kv_w, iq_w)
    in_w = w_in.shape[2]
    assert in_w == pool_w + attn_w + 2 * kv_w + iq_w + IDX_DIM + IDX_HEADS
    in_w_pad = in_w - IDX_HEADS + V7X_LANES

    x2 = x.reshape(m, d)
    c_pad = jnp.zeros((8, d), F32).at[:batch].set(c)
    for l in range(depth):
        mod = _ada_mod(c_pad, w_ada[l], b_ada[l][None, :], tn=1024)[:batch]
        shift1, scale1, gate1, shift2, scale2, gate2 = [
            mod[:, i * d:(i + 1) * d][:, None, :] for i in range(6)]

        w_in_p = jnp.zeros((d, in_w_pad), BF16).at[:, :in_w].set(w_in[l].astype(BF16))
        z_pool, q, k, v, iq, ik, iw = _in_proj(
            x2, g_pre_mix[l][None, :], scale1, shift1, w_in_p, tm=512, seq=seq, splits=splits)

        pool_n = _pool_mixer(z_pool, w_pool[l].astype(BF16), pool_scale[l][None, :], g_pool_out[l][None, :],
                             batch=batch, seq=seq, tt=512)
        attn_n = _dsa_attention(q, iq, iw, k, v, ik, g_attn_out[l][None, :],
                                batch=batch, seq=seq, tq=256, tk=512)

        x1, h2 = _out_proj(pool_n, attn_n, w_out[l].astype(BF16), x2, g_post_mix[l][None, :], gate1,
                           g_pre_ffn[l][None, :], scale2, shift2, tm=512, seq=seq)
        x2 = _ffn(h2, w_ff1[l].astype(BF16), w_ff2[l].astype(BF16), x1, g_post_ffn[l][None, :], gate2,
                  tm=512, tf=1024, seq=seq)
    return x2.reshape(batch, seq, d)
```

```python
import functools

import jax
import jax.numpy as jnp
from jax import lax
from jax.experimental import pallas as pl
from jax.experimental.pallas import tpu as pltpu

POOL_WINDOWS = (2, 4, 8, 16)
POOL_GROUPS = len(POOL_WINDOWS)
HEAD_DIM = 128
KV_HEADS = 2
IDX_HEADS = 16
IDX_DIM = 128
INDEX_TOPK = 256
RMS_EPS = 1e-6

V7X_VMEM_LIMIT_BYTES = 56 * 1024 * 1024

BF16 = jnp.bfloat16
F32 = jnp.float32
MAX_BISECT_STEPS = 320
NT_DIMS = (((1,), (1,)), ((), ()))
DSA_TQ = 256
DSA_TK = 512
SOFTMAX_EXP2_SCALE = (HEAD_DIM ** -0.5) * 1.4426950408889634
ONES_ROWS = 16
COUNT_CHAINS = 8
BISECT_STEPS_PER_CHECK = 2
VT_ROWS = KV_HEADS * (HEAD_DIM + ONES_ROWS)


def _rms(xf, g):
    return xf * lax.rsqrt(jnp.mean(xf * xf, axis=-1, keepdims=True) + RMS_EPS) * g


def _params(sem, flags=None):
    return pltpu.CompilerParams(dimension_semantics=sem, vmem_limit_bytes=V7X_VMEM_LIMIT_BYTES, flags=flags)


def _ada_kernel(c_ref, w_ref, b_ref, o_ref):
    c = c_ref[...]
    c_act = (c * jax.nn.sigmoid(c)).astype(BF16)
    o_ref[...] = jnp.dot(c_act, w_ref[...].astype(BF16), preferred_element_type=F32) + b_ref[...]


def _ada_mod(c_pad, w_ada, b_ada, *, tn):
    mp, d = c_pad.shape
    n = w_ada.shape[1]
    return pl.pallas_call(
        _ada_kernel,
        out_shape=jax.ShapeDtypeStruct((mp, n), F32),
        grid=(n // tn,),
        in_specs=[pl.BlockSpec((mp, d), lambda j: (0, 0)),
                  pl.BlockSpec((d, tn), lambda j: (0, j)),
                  pl.BlockSpec((1, tn), lambda j: (0, j))],
        out_specs=pl.BlockSpec((mp, tn), lambda j: (0, j)),
        compiler_params=_params(("arbitrary",)),
        name="ada_mod",
    )(c_pad, w_ada, b_ada)


def _in_proj_kernel(x_ref, g_ref, sc_ref, sh_ref, w_ref, wt_ref,
                    zp_ref, q_ref, k_ref, vt_ref, iq_ref, ik_ref, iwt_ref, *, splits):
    pool_w, attn_w, kv_w, iq_w = splits
    h = (_rms(x_ref[...], g_ref[...]) * (1.0 + sc_ref[0]) + sh_ref[0]).astype(BF16)

    def proj(lo, width):
        return jnp.dot(h, w_ref[:, lo:lo + width], preferred_element_type=F32)

    off = 0
    zp_ref[...] = proj(off, pool_w).astype(BF16)
    off += pool_w
    zq = proj(off, attn_w)
    for hd in range(attn_w // HEAD_DIM):
        q_ref[hd] = (zq[:, hd * HEAD_DIM:(hd + 1) * HEAD_DIM] * SOFTMAX_EXP2_SCALE).astype(BF16)
    off += attn_w
    k_ref[...] = proj(off, kv_w).astype(BF16)
    off += 2 * kv_w
    ziq = proj(off, iq_w)
    for hd in range(iq_w // IDX_DIM):
        iq_ref[hd] = ziq[:, hd * IDX_DIM:(hd + 1) * IDX_DIM].astype(BF16)
    off += iq_w
    ik_ref[...] = proj(off, IDX_DIM).astype(BF16)
    zt = lax.dot_general(wt_ref[...], h, NT_DIMS, preferred_element_type=F32)
    vt_blk = vt_ref.shape[2]
    ones = jnp.ones((ONES_ROWS, vt_blk), BF16)
    for cb in range(vt_ref.shape[0]):
        for g in range(KV_HEADS):
            base = g * (HEAD_DIM + ONES_ROWS)
            vt_ref[cb, base:base + HEAD_DIM, :] = zt[g * HEAD_DIM:(g + 1) * HEAD_DIM,
                                                     cb * vt_blk:(cb + 1) * vt_blk].astype(BF16)
            vt_ref[cb, base + HEAD_DIM:base + HEAD_DIM + ONES_ROWS, :] = ones
    iwt_ref[...] = zt[kv_w:kv_w + IDX_HEADS]


def _in_proj(x2, g, scale, shift, w_in, w_t, *, tm, seq, splits, vt_blk):
    m, d = x2.shape
    pool_w, attn_w, kv_w, iq_w = splits
    n_heads = attn_w // HEAD_DIM
    per_b = seq // tm
    assert tm % vt_blk == 0
    row = lambda i: (i, 0)
    col = lambda i: (0, i)
    hrow = lambda i: (0, i, 0)
    hrow0 = lambda i: (i, 0, 0)
    return pl.pallas_call(
        functools.partial(_in_proj_kernel, splits=splits),
        out_shape=(jax.ShapeDtypeStruct((m, pool_w), BF16),
                   jax.ShapeDtypeStruct((n_heads, m, HEAD_DIM), BF16),
                   jax.ShapeDtypeStruct((m, kv_w), BF16),
                   jax.ShapeDtypeStruct((m // vt_blk, VT_ROWS, vt_blk), BF16),
                   jax.ShapeDtypeStruct((IDX_HEADS, m, IDX_DIM), BF16),
                   jax.ShapeDtypeStruct((m, IDX_DIM), BF16),
                   jax.ShapeDtypeStruct((IDX_HEADS, m), F32)),
        grid=(m // tm,),
        in_specs=[pl.BlockSpec((tm, d), row),
                  pl.BlockSpec((1, d), lambda i: (0, 0)),
                  pl.BlockSpec((1, 1, d), lambda i: (i // per_b, 0, 0)),
                  pl.BlockSpec((1, 1, d), lambda i: (i // per_b, 0, 0)),
                  pl.BlockSpec(w_in.shape, lambda i: (0, 0), pipeline_mode=pl.Buffered(1)),
                  pl.BlockSpec(w_t.shape, lambda i: (0, 0), pipeline_mode=pl.Buffered(1))],
        out_specs=(pl.BlockSpec((tm, pool_w), row),
                   pl.BlockSpec((n_heads, tm, HEAD_DIM), hrow),
                   pl.BlockSpec((tm, kv_w), row),
                   pl.BlockSpec((tm // vt_blk, VT_ROWS, vt_blk), hrow0),
                   pl.BlockSpec((IDX_HEADS, tm, IDX_DIM), hrow),
                   pl.BlockSpec((tm, IDX_DIM), row),
                   pl.BlockSpec((IDX_HEADS, tm), col)),
        compiler_params=_params(("arbitrary",)),
        name="in_proj",
    )(x2, g, scale, shift, w_in, w_t)


def _pool_kernel(cur_ref, prev_ref, w_ref, ps_ref, g_ref, o_ref, ext_ref, mix_ref, *, halo):
    t = pl.program_id(1)
    tt = cur_ref.shape[0]
    ch = w_ref.shape[1]
    prev = prev_ref[...].astype(F32)
    ext_ref[0:halo, :] = jnp.where(t > 0, prev, jnp.zeros_like(prev))
    ext_ref[halo:halo + tt, :] = cur_ref[...].astype(F32)
    pos = t * tt + lax.broadcasted_iota(jnp.int32, (tt, 1), 0)
    for gi, win in enumerate(POOL_WINDOWS):
        cols = slice(gi * ch, (gi + 1) * ch)
        total = ext_ref[halo:halo + tt, cols]
        for j in range(1, win):
            total = total + ext_ref[halo - j:halo - j + tt, cols]
        inv_cnt = 1.0 / jnp.minimum(pos + 1, win).astype(F32)
        pooled = total * inv_cnt - ext_ref[halo:halo + tt, cols]
        mixed = jnp.dot(pooled.astype(BF16), w_ref[gi], preferred_element_type=F32)
        mix_ref[:, cols] = mixed * ps_ref[:, cols]
    o_ref[...] = _rms(mix_ref[...], g_ref[...]).astype(BF16)


def _pool_mixer(z_pool, w_pool, pool_scale, g_pool, *, batch, seq, tt):
    m, pw = z_pool.shape
    halo = max(POOL_WINDOWS)
    per_b = seq // tt
    hb = tt // halo
    return pl.pallas_call(
        functools.partial(_pool_kernel, halo=halo),
        out_shape=jax.ShapeDtypeStruct((m, pw), BF16),
        grid=(batch, per_b),
        in_specs=[pl.BlockSpec((tt, pw), lambda b, t: (b * per_b + t, 0)),
                  pl.BlockSpec((halo, pw), lambda b, t: (jnp.maximum((b * per_b + t) * hb - 1, 0), 0)),
                  pl.BlockSpec(w_pool.shape, lambda b, t: (0, 0, 0)),
                  pl.BlockSpec((1, pw), lambda b, t: (0, 0)),
                  pl.BlockSpec((1, pw), lambda b, t: (0, 0))],
        out_specs=pl.BlockSpec((tt, pw), lambda b, t: (b * per_b + t, 0)),
        scratch_shapes=[pltpu.VMEM((tt + halo, pw), F32), pltpu.VMEM((tt, pw), F32)],
        compiler_params=_params(("arbitrary", "arbitrary")),
        name="pool_mixer",
    )(z_pool, z_pool, w_pool, pool_scale, g_pool)


def _dsa_kernel(q_ref, iq_ref, iwt_ref, k_ref, vt_ref, ik_ref, g_ref, o_ref,
                sc_ref, bias_ref, acc_ref, m_ref, l_ref, *, tq, tk, topk, n_heads):
    qi = pl.program_id(1)
    t0 = qi * tq
    nkv = lax.div(t0 + tq + tk - 1, tk)
    q_per_kv = n_heads // KV_HEADS
    idx_scale = (IDX_HEADS ** -0.5) * (IDX_DIM ** -0.5)
    q_pos = t0 + lax.broadcasted_iota(jnp.int32, (tk, tq), 1)
    k_off = lax.broadcasted_iota(jnp.int32, (tk, tq), 0)
    w_idx = iwt_ref[...] * idx_scale

    def key_rows(kb):
        return pl.ds(pl.multiple_of(kb * tk, tk), tk)

    def score_block(kb, carry):
        lo, hi = carry
        ik = ik_ref[key_rows(kb), :]
        acc = jnp.zeros((tk, tq), F32)
        for h in range(IDX_HEADS):
            y = lax.dot_general(ik, iq_ref[h], NT_DIMS, preferred_element_type=F32)
            acc = acc + w_idx[h:h + 1, :] * jnp.maximum(y, 0.0)
        causal = (kb * tk + k_off) <= q_pos
        sc_ref[kb] = jnp.where(causal, acc, -jnp.inf)
        lo = jnp.minimum(lo, jnp.min(jnp.where(causal, acc, jnp.inf), axis=0, keepdims=True))
        hi = jnp.maximum(hi, jnp.max(jnp.where(causal, acc, -jnp.inf), axis=0, keepdims=True))
        return lo, hi

    lo0, hi0 = lax.fori_loop(0, nkv, score_block,
                             (jnp.full((1, tq), jnp.inf, F32), jnp.full((1, tq), -jnp.inf, F32)))

    def count_ge(cand):
        def body(kb, part):
            hit = jnp.where(sc_ref[kb] >= cand, 1.0, 0.0)
            return part + jnp.sum(hit.reshape(COUNT_CHAINS, tk // COUNT_CHAINS, tq), axis=0)
        part = lax.fori_loop(0, nkv, body, jnp.zeros((tk // COUNT_CHAINS, tq), F32))
        return jnp.sum(part, axis=0, keepdims=True)

    few_keys = (t0 + lax.broadcasted_iota(jnp.int32, (1, tq), 1)) < topk

    def bisect_cond(state):
        it, _, _, done = state
        return (it < MAX_BISECT_STEPS) & (jnp.min(done) < 0.5)

    def bisect_once(lo, hi, done):
        mid = 0.5 * lo + 0.5 * hi
        cnt = count_ge(mid)
        ge = cnt >= float(topk)
        move = (done < 0.5) & (mid > lo) & (mid < hi)
        new_lo = jnp.where(move & ge, mid, lo)
        new_hi = jnp.where(move & ~ge, mid, hi)
        new_done = jnp.where(move & (cnt != float(topk)), 0.0, 1.0)
        return new_lo, new_hi, new_done

    def bisect_step(state):
        it, lo, hi, done = state
        for _ in range(BISECT_STEPS_PER_CHECK):
            lo, hi, done = bisect_once(lo, hi, done)
        return it + BISECT_STEPS_PER_CHECK, lo, hi, done

    _, thr, _, _ = lax.while_loop(
        bisect_cond, bisect_step,
        (jnp.int32(0), lo0, hi0, jnp.where(few_keys, 1.0, 0.0)))

    def bias_block(kb, carry):
        bias_ref[kb] = jnp.where(sc_ref[kb] >= thr, 0.0, -jnp.inf)
        return carry

    lax.fori_loop(0, nkv, bias_block, 0)

    gq = q_per_kv * tq
    m_ref[...] = jnp.full(m_ref.shape, -jnp.inf, F32)
    l_ref[...] = jnp.zeros(l_ref.shape, F32)
    acc_ref[...] = jnp.zeros(acc_ref.shape, F32)

    def attn_block(kb, carry):
        rows = key_rows(kb)
        bias = bias_ref[kb]
        bias_g = jnp.concatenate([bias] * q_per_kv, axis=1)
        for g in range(KV_HEADS):
            kblk = k_ref[rows, g * HEAD_DIM:(g + 1) * HEAD_DIM]
            q_g = q_ref[g * q_per_kv:(g + 1) * q_per_kv].reshape(gq, HEAD_DIM)
            v_aug = vt_ref[kb, g * (HEAD_DIM + ONES_ROWS):(g + 1) * (HEAD_DIM + ONES_ROWS), :]
            s = lax.dot_general(kblk, q_g, NT_DIMS, preferred_element_type=F32) + bias_g
            m_run = m_ref[g:g + 1, :]
            m_new = jnp.maximum(m_run, jnp.max(s, axis=0, keepdims=True))
            m_safe = jnp.where(m_new == -jnp.inf, 0.0, m_new)
            alpha = jnp.exp2(m_run - m_safe)
            p = jnp.exp2(s - m_safe).astype(BF16)
            pv = jnp.dot(v_aug, p, preferred_element_type=F32)
            arows = slice(g * HEAD_DIM, (g + 1) * HEAD_DIM)
            acc_ref[arows, :] = alpha * acc_ref[arows, :] + pv[:HEAD_DIM]
            l_ref[g:g + 1, :] = alpha * l_ref[g:g + 1, :] + pv[HEAD_DIM:HEAD_DIM + 1]
            m_ref[g:g + 1, :] = m_new
        return carry

    lax.fori_loop(0, nkv, attn_block, 0)

    def head_view(hd):
        g, j = divmod(hd, q_per_kv)
        return g, slice(g * HEAD_DIM, (g + 1) * HEAD_DIM), slice(j * tq, (j + 1) * tq)

    ssq = jnp.zeros((1, tq), F32)
    for hd in range(n_heads):
        g, arows, lanes = head_view(hd)
        o_h = acc_ref[arows, lanes] / l_ref[g:g + 1, lanes]
        acc_ref[arows, lanes] = o_h
        ssq = ssq + jnp.sum(o_h * o_h, axis=0, keepdims=True)
    inv_rms = lax.rsqrt(ssq / float(n_heads * HEAD_DIM) + RMS_EPS)
    for hd in range(n_heads):
        _, arows, lanes = head_view(hd)
        o_t = (acc_ref[arows, lanes] * inv_rms).T
        cols = slice(hd * HEAD_DIM, (hd + 1) * HEAD_DIM)
        o_ref[:, cols] = (o_t * g_ref[:, cols]).astype(BF16)


def _dsa_attention(q, iq, iwt, k, vt, ik, g_attn, *, batch, seq, tq, tk):
    n_heads, m, _ = q.shape
    attn_w = n_heads * HEAD_DIM
    kv_w = k.shape[1]
    per_b = seq // tq
    topk = min(INDEX_TOPK, seq // 4)
    gq = (n_heads // KV_HEADS) * tq
    qmap = lambda b, i: (0, b * per_b + i, 0)
    return pl.pallas_call(
        functools.partial(_dsa_kernel, tq=tq, tk=tk, topk=topk, n_heads=n_heads),
        out_shape=jax.ShapeDtypeStruct((m, attn_w), BF16),
        grid=(batch, per_b),
        in_specs=[pl.BlockSpec((n_heads, tq, HEAD_DIM), qmap),
                  pl.BlockSpec((IDX_HEADS, tq, IDX_DIM), qmap),
                  pl.BlockSpec((IDX_HEADS, tq), lambda b, i: (0, b * per_b + i)),
                  pl.BlockSpec((seq, kv_w), lambda b, i: (b, 0)),
                  pl.BlockSpec((seq // tk, VT_ROWS, tk), lambda b, i: (b, 0, 0)),
                  pl.BlockSpec((seq, IDX_DIM), lambda b, i: (b, 0)),
                  pl.BlockSpec((1, attn_w), lambda b, i: (0, 0))],
        out_specs=pl.BlockSpec((tq, attn_w), lambda b, i: (b * per_b + i, 0)),
        scratch_shapes=[pltpu.VMEM((seq // tk, tk, tq), F32),
                        pltpu.VMEM((seq // tk, tk, tq), F32),
                        pltpu.VMEM((KV_HEADS * HEAD_DIM, gq), F32),
                        pltpu.VMEM((8, gq), F32),
                        pltpu.VMEM((8, gq), F32)],
        compiler_params=_params(("arbitrary", "arbitrary")),
        name="dsa_attn",
    )(q, iq, iwt, k, vt, ik, g_attn)


def _out_proj_kernel(pool_ref, attn_ref, w_ref, x_ref, gpost_ref, gate_ref, gpre_ref, sc_ref, sh_ref,
                     x1_ref, h2_ref):
    pw = pool_ref.shape[1]
    mix = jnp.dot(pool_ref[...], w_ref[0:pw, :], preferred_element_type=F32)
    mix = mix + jnp.dot(attn_ref[...], w_ref[pw:, :], preferred_element_type=F32)
    x1 = x_ref[...] + gate_ref[0] * _rms(mix, gpost_ref[...])
    x1_ref[...] = x1
    h2_ref[...] = (_rms(x1, gpre_ref[...]) * (1.0 + sc_ref[0]) + sh_ref[0]).astype(BF16)


def _out_proj(pool_n, attn_n, w_out, x2, g_post, gate, g_pre, scale, shift, *, tm, seq):
    m, d = x2.shape
    pw, aw = pool_n.shape[1], attn_n.shape[1]
    per_b = seq // tm
    row = lambda i: (i, 0)
    vec = lambda i: (0, 0)
    bvec = lambda i: (i // per_b, 0, 0)
    return pl.pallas_call(
        _out_proj_kernel,
        out_shape=(jax.ShapeDtypeStruct((m, d), F32), jax.ShapeDtypeStruct((m, d), BF16)),
        grid=(m // tm,),
        in_specs=[pl.BlockSpec((tm, pw), row),
                  pl.BlockSpec((tm, aw), row),
                  pl.BlockSpec(w_out.shape, vec, pipeline_mode=pl.Buffered(1)),
                  pl.BlockSpec((tm, d), row),
                  pl.BlockSpec((1, d), vec),
                  pl.BlockSpec((1, 1, d), bvec),
                  pl.BlockSpec((1, d), vec),
                  pl.BlockSpec((1, 1, d), bvec),
                  pl.BlockSpec((1, 1, d), bvec)],
        out_specs=(pl.BlockSpec((tm, d), row), pl.BlockSpec((tm, d), row)),
        compiler_params=_params(("arbitrary",)),
        name="out_proj",
    )(pool_n, attn_n, w_out, x2, g_post, gate, g_pre, scale, shift)


def _ffn_kernel(h_ref, w1_ref, w2_ref, x_ref, g_ref, gate_ref, o_ref, acc_ref):
    j = pl.program_id(1)

    @pl.when(j == 0)
    def _():
        acc_ref[...] = jnp.zeros_like(acc_ref)

    a = jnp.dot(h_ref[...], w1_ref[...], preferred_element_type=F32)
    a = jnp.square(jnp.maximum(a, 0.0)).astype(BF16)
    acc_ref[...] += jnp.dot(a, w2_ref[...], preferred_element_type=F32)

    @pl.when(j == pl.num_programs(1) - 1)
    def _():
        o_ref[...] = x_ref[...] + gate_ref[0] * _rms(acc_ref[...], g_ref[...])


def _ffn(h2, w1, w2, x1, g_post, gate, *, tm, tf, seq):
    m, d = x1.shape
    f = w1.shape[1]
    per_b = seq // tm
    return pl.pallas_call(
        _ffn_kernel,
        out_shape=jax.ShapeDtypeStruct((m, d), F32),
        grid=(m // tm, f // tf),
        in_specs=[pl.BlockSpec((tm, d), lambda i, j: (i, 0)),
                  pl.BlockSpec((d, tf), lambda i, j: (0, j)),
                  pl.BlockSpec((tf, d), lambda i, j: (j, 0)),
                  pl.BlockSpec((tm, d), lambda i, j: (i, 0)),
                  pl.BlockSpec((1, d), lambda i, j: (0, 0)),
                  pl.BlockSpec((1, 1, d), lambda i, j: (i // per_b, 0, 0))],
        out_specs=pl.BlockSpec((tm, d), lambda i, j: (i, 0)),
        scratch_shapes=[pltpu.VMEM((tm, d), F32)],
        compiler_params=_params(("arbitrary", "arbitrary")),
        name="ffn",
    )(h2, w1, w2, x1, g_post, gate)


def kernel(x, c, w_ada, b_ada, g_pre_mix, g_post_mix, g_pre_ffn, g_post_ffn, w_in, w_pool, pool_scale,
           g_pool_out, g_attn_out, w_out, w_ff1, w_ff2):
    batch, seq, d = x.shape
    depth = w_ada.shape[0]
    m = batch * seq
    pool_w = pool_scale.shape[1]
    attn_w = g_attn_out.shape[1]
    kv_w = KV_HEADS * HEAD_DIM
    iq_w = IDX_HEADS * IDX_DIM
    splits = (pool_w, attn_w, kv_w, iq_w)
    in_w = w_in.shape[2]
    assert in_w == pool_w + attn_w + 2 * kv_w + iq_w + IDX_DIM + IDX_HEADS
    v_lo = pool_w + attn_w + kv_w

    x2 = x.reshape(m, d)
    c_pad = jnp.zeros((8, d), F32).at[:batch].set(c)
    for l in range(depth):
        mod = _ada_mod(c_pad, w_ada[l], b_ada[l][None, :], tn=1024)[:batch]
        shift1, scale1, gate1, shift2, scale2, gate2 = [
            mod[:, i * d:(i + 1) * d][:, None, :] for i in range(6)]

        w_in_b = w_in[l].astype(BF16)
        w_t = jnp.concatenate([w_in_b[:, v_lo:v_lo + kv_w], w_in_b[:, in_w - IDX_HEADS:]], axis=1).T
        z_pool, q, k, vt, iq, ik, iwt = _in_proj(
            x2, g_pre_mix[l][None, :], scale1, shift1, w_in_b, w_t, tm=512, seq=seq, splits=splits,
            vt_blk=DSA_TK)

        pool_n = _pool_mixer(z_pool, w_pool[l].astype(BF16), pool_scale[l][None, :], g_pool_out[l][None, :],
                             batch=batch, seq=seq, tt=512)
        attn_n = _dsa_attention(q, iq, iwt, k, vt, ik, g_attn_out[l][None, :],
                                batch=batch, seq=seq, tq=DSA_TQ, tk=DSA_TK)

        x1, h2 = _out_proj(pool_n, attn_n, w_out[l].astype(BF16), x2, g_post_mix[l][None, :], gate1,
                           g_pre_ffn[l][None, :], scale2, shift2, tm=512, seq=seq)
        x2 = _ffn(h2, w_ff1[l].astype(BF16), w_ff2[l].astype(BF16), x1, g_post_ffn[l][None, :], gate2,
                  tm=512, tf=1024, seq=seq)
    return x2.reshape(batch, seq, d)
```

```python
import functools

import jax
import jax.numpy as jnp
from jax import lax
from jax.experimental import pallas as pl
from jax.experimental.pallas import tpu as pltpu

POOL_WINDOWS = (2, 4, 8, 16)
POOL_GROUPS = len(POOL_WINDOWS)
HEAD_DIM = 128
KV_HEADS = 2
IDX_HEADS = 16
IDX_DIM = 128
INDEX_TOPK = 256
RMS_EPS = 1e-6

V7X_VMEM_LIMIT_BYTES = 56 * 1024 * 1024

BF16 = jnp.bfloat16
F32 = jnp.float32
MAX_BISECT_STEPS = 320
NT_DIMS = (((1,), (1,)), ((), ()))
DSA_TQ = 256
DSA_TK = 512
DSA_SUB = 256
SOFTMAX_EXP2_SCALE = (HEAD_DIM ** -0.5) * 1.4426950408889634
ONES_ROWS = 16
COUNT_CHAINS = 8
BISECT_STEPS_PER_CHECK = 2
VT_ROWS = KV_HEADS * (HEAD_DIM + ONES_ROWS)


def _rms(xf, g):
    return xf * lax.rsqrt(jnp.mean(xf * xf, axis=-1, keepdims=True) + RMS_EPS) * g


def _params(sem, flags=None):
    return pltpu.CompilerParams(dimension_semantics=sem, vmem_limit_bytes=V7X_VMEM_LIMIT_BYTES, flags=flags)


def _ada_kernel(c_ref, w_ref, b_ref, o_ref):
    c = c_ref[...]
    c_act = (c * jax.nn.sigmoid(c)).astype(BF16)
    o_ref[...] = jnp.dot(c_act, w_ref[...].astype(BF16), preferred_element_type=F32) + b_ref[...]


def _ada_mod(c_pad, w_ada, b_ada, *, tn):
    mp, d = c_pad.shape
    n = w_ada.shape[1]
    return pl.pallas_call(
        _ada_kernel,
        out_shape=jax.ShapeDtypeStruct((mp, n), F32),
        grid=(n // tn,),
        in_specs=[pl.BlockSpec((mp, d), lambda j: (0, 0)),
                  pl.BlockSpec((d, tn), lambda j: (0, j)),
                  pl.BlockSpec((1, tn), lambda j: (0, j))],
        out_specs=pl.BlockSpec((mp, tn), lambda j: (0, j)),
        compiler_params=_params(("arbitrary",)),
        name="ada_mod",
    )(c_pad, w_ada, b_ada)


def _in_proj_kernel(x_ref, g_ref, sc_ref, sh_ref, w_ref, wt_ref,
                    zp_ref, q_ref, k_ref, vt_ref, iq_ref, ik_ref, iwt_ref, *, splits):
    pool_w, attn_w, kv_w, iq_w = splits
    h = (_rms(x_ref[...], g_ref[...]) * (1.0 + sc_ref[0]) + sh_ref[0]).astype(BF16)

    def proj(lo, width):
        return jnp.dot(h, w_ref[:, lo:lo + width], preferred_element_type=F32)

    off = 0
    zp_ref[...] = proj(off, pool_w).astype(BF16)
    off += pool_w
    zq = proj(off, attn_w)
    for hd in range(attn_w // HEAD_DIM):
        q_ref[hd] = (zq[:, hd * HEAD_DIM:(hd + 1) * HEAD_DIM] * SOFTMAX_EXP2_SCALE).astype(BF16)
    off += attn_w
    k_ref[...] = proj(off, kv_w).astype(BF16)
    off += 2 * kv_w
    ziq = proj(off, iq_w)
    for hd in range(iq_w // IDX_DIM):
        iq_ref[hd] = ziq[:, hd * IDX_DIM:(hd + 1) * IDX_DIM].astype(BF16)
    off += iq_w
    ik_ref[...] = proj(off, IDX_DIM).astype(BF16)
    zt = lax.dot_general(wt_ref[...], h, NT_DIMS, preferred_element_type=F32)
    vt_blk = vt_ref.shape[2]
    ones = jnp.ones((ONES_ROWS, vt_blk), BF16)
    for cb in range(vt_ref.shape[0]):
        for g in range(KV_HEADS):
            base = g * (HEAD_DIM + ONES_ROWS)
            vt_ref[cb, base:base + HEAD_DIM, :] = zt[g * HEAD_DIM:(g + 1) * HEAD_DIM,
                                                     cb * vt_blk:(cb + 1) * vt_blk].astype(BF16)
            vt_ref[cb, base + HEAD_DIM:base + HEAD_DIM + ONES_ROWS, :] = ones
    iwt_ref[...] = zt[kv_w:kv_w + IDX_HEADS]


def _in_proj(x2, g, scale, shift, w_in, w_t, *, tm, seq, splits, vt_blk):
    m, d = x2.shape
    pool_w, attn_w, kv_w, iq_w = splits
    n_heads = attn_w // HEAD_DIM
    per_b = seq // tm
    assert tm % vt_blk == 0
    row = lambda i: (i, 0)
    col = lambda i: (0, i)
    hrow = lambda i: (0, i, 0)
    hrow0 = lambda i: (i, 0, 0)
    return pl.pallas_call(
        functools.partial(_in_proj_kernel, splits=splits),
        out_shape=(jax.ShapeDtypeStruct((m, pool_w), BF16),
                   jax.ShapeDtypeStruct((n_heads, m, HEAD_DIM), BF16),
                   jax.ShapeDtypeStruct((m, kv_w), BF16),
                   jax.ShapeDtypeStruct((m // vt_blk, VT_ROWS, vt_blk), BF16),
                   jax.ShapeDtypeStruct((IDX_HEADS, m, IDX_DIM), BF16),
                   jax.ShapeDtypeStruct((m, IDX_DIM), BF16),
                   jax.ShapeDtypeStruct((IDX_HEADS, m), F32)),
        grid=(m // tm,),
        in_specs=[pl.BlockSpec((tm, d), row),
                  pl.BlockSpec((1, d), lambda i: (0, 0)),
                  pl.BlockSpec((1, 1, d), lambda i: (i // per_b, 0, 0)),
                  pl.BlockSpec((1, 1, d), lambda i: (i // per_b, 0, 0)),
                  pl.BlockSpec(w_in.shape, lambda i: (0, 0), pipeline_mode=pl.Buffered(1)),
                  pl.BlockSpec(w_t.shape, lambda i: (0, 0), pipeline_mode=pl.Buffered(1))],
        out_specs=(pl.BlockSpec((tm, pool_w), row),
                   pl.BlockSpec((n_heads, tm, HEAD_DIM), hrow),
                   pl.BlockSpec((tm, kv_w), row),
                   pl.BlockSpec((tm // vt_blk, VT_ROWS, vt_blk), hrow0),
                   pl.BlockSpec((IDX_HEADS, tm, IDX_DIM), hrow),
                   pl.BlockSpec((tm, IDX_DIM), row),
                   pl.BlockSpec((IDX_HEADS, tm), col)),
        compiler_params=_params(("arbitrary",)),
        name="in_proj",
    )(x2, g, scale, shift, w_in, w_t)


def _pool_kernel(cur_ref, prev_ref, w_ref, ps_ref, g_ref, o_ref, ext_ref, mix_ref, *, halo):
    t = pl.program_id(1)
    tt = cur_ref.shape[0]
    ch = w_ref.shape[1]
    prev = prev_ref[...].astype(F32)
    ext_ref[0:halo, :] = jnp.where(t > 0, prev, jnp.zeros_like(prev))
    ext_ref[halo:halo + tt, :] = cur_ref[...].astype(F32)
    pos = t * tt + lax.broadcasted_iota(jnp.int32, (tt, 1), 0)
    for gi, win in enumerate(POOL_WINDOWS):
        cols = slice(gi * ch, (gi + 1) * ch)
        total = ext_ref[halo:halo + tt, cols]
        for j in range(1, win):
            total = total + ext_ref[halo - j:halo - j + tt, cols]
        inv_cnt = 1.0 / jnp.minimum(pos + 1, win).astype(F32)
        pooled = total * inv_cnt - ext_ref[halo:halo + tt, cols]
        mixed = jnp.dot(pooled.astype(BF16), w_ref[gi], preferred_element_type=F32)
        mix_ref[:, cols] = mixed * ps_ref[:, cols]
    o_ref[...] = _rms(mix_ref[...], g_ref[...]).astype(BF16)


def _pool_mixer(z_pool, w_pool, pool_scale, g_pool, *, batch, seq, tt):
    m, pw = z_pool.shape
    halo = max(POOL_WINDOWS)
    per_b = seq // tt
    hb = tt // halo
    return pl.pallas_call(
        functools.partial(_pool_kernel, halo=halo),
        out_shape=jax.ShapeDtypeStruct((m, pw), BF16),
        grid=(batch, per_b),
        in_specs=[pl.BlockSpec((tt, pw), lambda b, t: (b * per_b + t, 0)),
                  pl.BlockSpec((halo, pw), lambda b, t: (jnp.maximum((b * per_b + t) * hb - 1, 0), 0)),
                  pl.BlockSpec(w_pool.shape, lambda b, t: (0, 0, 0)),
                  pl.BlockSpec((1, pw), lambda b, t: (0, 0)),
                  pl.BlockSpec((1, pw), lambda b, t: (0, 0))],
        out_specs=pl.BlockSpec((tt, pw), lambda b, t: (b * per_b + t, 0)),
        scratch_shapes=[pltpu.VMEM((tt + halo, pw), F32), pltpu.VMEM((tt, pw), F32)],
        compiler_params=_params(("arbitrary", "arbitrary")),
        name="pool_mixer",
    )(z_pool, z_pool, w_pool, pool_scale, g_pool)


def _dsa_kernel(q_ref, iq_ref, iwt_ref, k_ref, vt_ref, ik_ref, g_ref, *rest, tq, tk, topk, n_heads, n_cast):
    cast_in, (o_ref,), cast_out = rest[:n_cast], rest[n_cast:n_cast + 1], rest[n_cast + 1:2 * n_cast + 1]
    sc_ref, bias_ref, acc_ref, m_ref, l_ref, s_ref = rest[2 * n_cast + 1:]
    for src_ref, dst_ref in zip(cast_in, cast_out):
        dst_ref[...] = src_ref[...].astype(BF16)
    qi = pl.program_id(1)
    t0 = qi * tq
    nkv = lax.div(t0 + tq + tk - 1, tk)
    q_per_kv = n_heads // KV_HEADS
    idx_scale = (IDX_HEADS ** -0.5) * (IDX_DIM ** -0.5)
    q_pos = t0 + lax.broadcasted_iota(jnp.int32, (tk, tq), 1)
    k_off = lax.broadcasted_iota(jnp.int32, (tk, tq), 0)
    w_idx = iwt_ref[...] * idx_scale

    sub = sc_ref.shape[1]
    sub_per_blk = tk // sub
    n_sub = lax.div(t0 + tq + sub - 1, sub)

    def key_rows(kb):
        return pl.ds(pl.multiple_of(kb * tk, tk), tk)

    def score_block(kb, carry):
        lo, hi = carry
        ik = ik_ref[key_rows(kb), :]
        acc = jnp.zeros((tk, tq), F32)
        for h in range(IDX_HEADS):
            y = lax.dot_general(ik, iq_ref[h], NT_DIMS, preferred_element_type=F32)
            acc = acc + w_idx[h:h + 1, :] * jnp.maximum(y, 0.0)
        causal = (kb * tk + k_off) <= q_pos
        masked = jnp.where(causal, acc, -jnp.inf)
        for c in range(sub_per_blk):
            sc_ref[kb * sub_per_blk + c] = masked[c * sub:(c + 1) * sub]
        lo = jnp.minimum(lo, jnp.min(jnp.where(causal, acc, jnp.inf), axis=0, keepdims=True))
        hi = jnp.maximum(hi, jnp.max(jnp.where(causal, acc, -jnp.inf), axis=0, keepdims=True))
        return lo, hi

    lo0, hi0 = lax.fori_loop(0, nkv, score_block,
                             (jnp.full((1, tq), jnp.inf, F32), jnp.full((1, tq), -jnp.inf, F32)))

    def count_ge(cand):
        def body(sb, part):
            hit = jnp.where(sc_ref[sb] >= cand, 1.0, 0.0)
            return part + jnp.sum(hit.reshape(COUNT_CHAINS, sub // COUNT_CHAINS, tq), axis=0)
        part = lax.fori_loop(0, n_sub, body, jnp.zeros((sub // COUNT_CHAINS, tq), F32))
        return jnp.sum(part, axis=0, keepdims=True)

    few_keys = (t0 + lax.broadcasted_iota(jnp.int32, (1, tq), 1)) < topk

    def bisect_cond(state):
        it, _, _, done = state
        return (it < MAX_BISECT_STEPS) & (jnp.min(done) < 0.5)

    def bisect_once(lo, hi, done):
        mid = 0.5 * lo + 0.5 * hi
        cnt = count_ge(mid)
        ge = cnt >= float(topk)
        move = (done < 0.5) & (mid > lo) & (mid < hi)
        new_lo = jnp.where(move & ge, mid, lo)
        new_hi = jnp.where(move & ~ge, mid, hi)
        new_done = jnp.where(move & (cnt != float(topk)), 0.0, 1.0)
        return new_lo, new_hi, new_done

    def bisect_step(state):
        it, lo, hi, done = state
        for _ in range(BISECT_STEPS_PER_CHECK):
            lo, hi, done = bisect_once(lo, hi, done)
        return it + BISECT_STEPS_PER_CHECK, lo, hi, done

    _, thr, _, _ = lax.while_loop(
        bisect_cond, bisect_step,
        (jnp.int32(0), lo0, hi0, jnp.where(few_keys, 1.0, 0.0)))

    def bias_block(sb, carry):
        bias_ref[sb] = jnp.where(sc_ref[sb] >= thr, 0.0, -jnp.inf)
        return carry

    lax.fori_loop(0, nkv * sub_per_blk, bias_block, 0)

    gq = q_per_kv * tq
    m_ref[...] = jnp.full(m_ref.shape, -jnp.inf, F32)
    l_ref[...] = jnp.zeros(l_ref.shape, F32)
    acc_ref[...] = jnp.zeros(acc_ref.shape, F32)

    def attn_block(kb, carry):
        rows = key_rows(kb)
        bias = jnp.concatenate([bias_ref[kb * sub_per_blk + c] for c in range(sub_per_blk)], axis=0)
        bias_g = jnp.concatenate([bias] * q_per_kv, axis=1)
        for g in range(KV_HEADS):
            kblk = k_ref[rows, g * HEAD_DIM:(g + 1) * HEAD_DIM]
            q_g = q_ref[g * q_per_kv:(g + 1) * q_per_kv].reshape(gq, HEAD_DIM)
            s_ref[g] = lax.dot_general(kblk, q_g, NT_DIMS, preferred_element_type=F32)
        for g in range(KV_HEADS):
            v_aug = vt_ref[kb, g * (HEAD_DIM + ONES_ROWS):(g + 1) * (HEAD_DIM + ONES_ROWS), :]
            s = s_ref[g] + bias_g
            m_run = m_ref[g:g + 1, :]
            m_new = jnp.maximum(m_run, jnp.max(s, axis=0, keepdims=True))
            m_safe = jnp.where(m_new == -jnp.inf, 0.0, m_new)
            alpha = jnp.exp2(m_run - m_safe)
            p = jnp.exp2(s - m_safe).astype(BF16)
            pv = jnp.dot(v_aug, p, preferred_element_type=F32)
            arows = slice(g * HEAD_DIM, (g + 1) * HEAD_DIM)
            acc_ref[arows, :] = alpha * acc_ref[arows, :] + pv[:HEAD_DIM]
            l_ref[g:g + 1, :] = alpha * l_ref[g:g + 1, :] + pv[HEAD_DIM:HEAD_DIM + 1]
            m_ref[g:g + 1, :] = m_new
        return carry

    lax.fori_loop(0, nkv, attn_block, 0)

    def head_view(hd):
        g, j = divmod(hd, q_per_kv)
        return g, slice(g * HEAD_DIM, (g + 1) * HEAD_DIM), slice(j * tq, (j + 1) * tq)

    ssq = jnp.zeros((1, tq), F32)
    for hd in range(n_heads):
        g, arows, lanes = head_view(hd)
        o_h = acc_ref[arows, lanes] / l_ref[g:g + 1, lanes]
        acc_ref[arows, lanes] = o_h
        ssq = ssq + jnp.sum(o_h * o_h, axis=0, keepdims=True)
    inv_rms = lax.rsqrt(ssq / float(n_heads * HEAD_DIM) + RMS_EPS)
    for hd in range(n_heads):
        _, arows, lanes = head_view(hd)
        o_t = (acc_ref[arows, lanes] * inv_rms).T
        cols = slice(hd * HEAD_DIM, (hd + 1) * HEAD_DIM)
        o_ref[:, cols] = (o_t * g_ref[:, cols]).astype(BF16)


def _dsa_attention(q, iq, iwt, k, vt, ik, g_attn, cast_weights, *, batch, seq, tq, tk):
    n_heads, m, _ = q.shape
    attn_w = n_heads * HEAD_DIM
    kv_w = k.shape[1]
    per_b = seq // tq
    n_steps = batch * per_b
    topk = min(INDEX_TOPK, seq // 4)
    gq = (n_heads // KV_HEADS) * tq
    qmap = lambda b, i: (0, b * per_b + i, 0)
    cast_specs, cast_shapes = [], []
    for w, axis in cast_weights:
        assert w.shape[axis] % n_steps == 0 and (w.shape[axis] // n_steps) % (16 if axis == 0 else 128) == 0
        blk = list(w.shape)
        blk[axis] //= n_steps
        imap = (lambda b, i: (b * per_b + i, 0)) if axis == 0 else (lambda b, i: (0, b * per_b + i))
        cast_specs.append(pl.BlockSpec(tuple(blk), imap))
        cast_shapes.append(jax.ShapeDtypeStruct(w.shape, BF16))
    n_cast = len(cast_weights)
    outs = pl.pallas_call(
        functools.partial(_dsa_kernel, tq=tq, tk=tk, topk=topk, n_heads=n_heads, n_cast=n_cast),
        out_shape=[jax.ShapeDtypeStruct((m, attn_w), BF16)] + cast_shapes,
        grid=(batch, per_b),
        in_specs=[pl.BlockSpec((n_heads, tq, HEAD_DIM), qmap),
                  pl.BlockSpec((IDX_HEADS, tq, IDX_DIM), qmap),
                  pl.BlockSpec((IDX_HEADS, tq), lambda b, i: (0, b * per_b + i)),
                  pl.BlockSpec((seq, kv_w), lambda b, i: (b, 0)),
                  pl.BlockSpec((seq // tk, VT_ROWS, tk), lambda b, i: (b, 0, 0)),
                  pl.BlockSpec((seq, IDX_DIM), lambda b, i: (b, 0)),
                  pl.BlockSpec((1, attn_w), lambda b, i: (0, 0))] + cast_specs,
        out_specs=[pl.BlockSpec((tq, attn_w), lambda b, i: (b * per_b + i, 0))] + cast_specs,
        scratch_shapes=[pltpu.VMEM((seq // DSA_SUB, DSA_SUB, tq), F32),
                        pltpu.VMEM((seq // DSA_SUB, DSA_SUB, tq), F32),
                        pltpu.VMEM((KV_HEADS * HEAD_DIM, gq), F32),
                        pltpu.VMEM((8, gq), F32),
                        pltpu.VMEM((8, gq), F32),
                        pltpu.VMEM((KV_HEADS, tk, gq), F32)],
        compiler_params=_params(("arbitrary", "arbitrary")),
        name="dsa_attn",
    )(q, iq, iwt, k, vt, ik, g_attn, *[w for w, _ in cast_weights])
    return outs[0], outs[1:]


def _out_proj_kernel(pool_ref, attn_ref, w_ref, x_ref, gpost_ref, gate_ref, gpre_ref, sc_ref, sh_ref,
                     x1_ref, h2_ref):
    pw = pool_ref.shape[1]
    mix = jnp.dot(pool_ref[...], w_ref[0:pw, :], preferred_element_type=F32)
    mix = mix + jnp.dot(attn_ref[...], w_ref[pw:, :], preferred_element_type=F32)
    x1 = x_ref[...] + gate_ref[0] * _rms(mix, gpost_ref[...])
    x1_ref[...] = x1
    h2_ref[...] = (_rms(x1, gpre_ref[...]) * (1.0 + sc_ref[0]) + sh_ref[0]).astype(BF16)


def _out_proj(pool_n, attn_n, w_out, x2, g_post, gate, g_pre, scale, shift, *, tm, seq):
    m, d = x2.shape
    pw, aw = pool_n.shape[1], attn_n.shape[1]
    per_b = seq // tm
    row = lambda i: (i, 0)
    vec = lambda i: (0, 0)
    bvec = lambda i: (i // per_b, 0, 0)
    return pl.pallas_call(
        _out_proj_kernel,
        out_shape=(jax.ShapeDtypeStruct((m, d), F32), jax.ShapeDtypeStruct((m, d), BF16)),
        grid=(m // tm,),
        in_specs=[pl.BlockSpec((tm, pw), row),
                  pl.BlockSpec((tm, aw), row),
                  pl.BlockSpec(w_out.shape, vec, pipeline_mode=pl.Buffered(1)),
                  pl.BlockSpec((tm, d), row),
                  pl.BlockSpec((1, d), vec),
                  pl.BlockSpec((1, 1, d), bvec),
                  pl.BlockSpec((1, d), vec),
                  pl.BlockSpec((1, 1, d), bvec),
                  pl.BlockSpec((1, 1, d), bvec)],
        out_specs=(pl.BlockSpec((tm, d), row), pl.BlockSpec((tm, d), row)),
        compiler_params=_params(("arbitrary",)),
        name="out_proj",
    )(pool_n, attn_n, w_out, x2, g_post, gate, g_pre, scale, shift)


def _ffn_kernel(h_ref, w1_ref, w2_ref, x_ref, g_ref, gate_ref, o_ref, acc_ref):
    j = pl.program_id(1)
    a = jnp.dot(h_ref[...], w1_ref[...], preferred_element_type=F32)
    a = jnp.square(jnp.maximum(a, 0.0)).astype(BF16)
    part = jnp.dot(a, w2_ref[...], preferred_element_type=F32)
    acc_ref[...] = jnp.where(j == 0, part, acc_ref[...] + part)

    @pl.when(j == pl.num_programs(1) - 1)
    def _():
        o_ref[...] = x_ref[...] + gate_ref[0] * _rms(acc_ref[...], g_ref[...])


def _ffn(h2, w1, w2, x1, g_post, gate, *, tm, tf, seq):
    m, d = x1.shape
    f = w1.shape[1]
    per_b = seq // tm
    return pl.pallas_call(
        _ffn_kernel,
        out_shape=jax.ShapeDtypeStruct((m, d), F32),
        grid=(m // tm, f // tf),
        in_specs=[pl.BlockSpec((tm, d), lambda i, j: (i, 0)),
                  pl.BlockSpec((d, tf), lambda i, j: (0, j)),
                  pl.BlockSpec((tf, d), lambda i, j: (j, 0)),
                  pl.BlockSpec((tm, d), lambda i, j: (i, 0)),
                  pl.BlockSpec((1, d), lambda i, j: (0, 0)),
                  pl.BlockSpec((1, 1, d), lambda i, j: (i // per_b, 0, 0))],
        out_specs=pl.BlockSpec((tm, d), lambda i, j: (i, 0)),
        scratch_shapes=[pltpu.VMEM((tm, d), F32)],
        compiler_params=_params(("arbitrary", "arbitrary")),
        name="ffn",
    )(h2, w1, w2, x1, g_post, gate)


def kernel(x, c, w_ada, b_ada, g_pre_mix, g_post_mix, g_pre_ffn, g_post_ffn, w_in, w_pool, pool_scale,
           g_pool_out, g_attn_out, w_out, w_ff1, w_ff2):
    batch, seq, d = x.shape
    depth = w_ada.shape[0]
    m = batch * seq
    pool_w = pool_scale.shape[1]
    attn_w = g_attn_out.shape[1]
    kv_w = KV_HEADS * HEAD_DIM
    iq_w = IDX_HEADS * IDX_DIM
    splits = (pool_w, attn_w, kv_w, iq_w)
    in_w = w_in.shape[2]
    assert in_w == pool_w + attn_w + 2 * kv_w + iq_w + IDX_DIM + IDX_HEADS
    v_lo = pool_w + attn_w + kv_w

    x2 = x.reshape(m, d)
    c_pad = jnp.zeros((8, d), F32).at[:batch].set(c)
    for l in range(depth):
        mod = _ada_mod(c_pad, w_ada[l], b_ada[l][None, :], tn=1024)[:batch]
        shift1, scale1, gate1, shift2, scale2, gate2 = [
            mod[:, i * d:(i + 1) * d][:, None, :] for i in range(6)]

        w_in_b = w_in[l].astype(BF16)
        w_t = jnp.concatenate([w_in[l][:, v_lo:v_lo + kv_w], w_in[l][:, in_w - IDX_HEADS:]], axis=1).T.astype(BF16)
        z_pool, q, k, vt, iq, ik, iwt = _in_proj(
            x2, g_pre_mix[l][None, :], scale1, shift1, w_in_b, w_t, tm=512, seq=seq, splits=splits,
            vt_blk=DSA_TK)

        pool_n = _pool_mixer(z_pool, w_pool[l].astype(BF16), pool_scale[l][None, :], g_pool_out[l][None, :],
                             batch=batch, seq=seq, tt=512)
        attn_n, (w_out_b, w_ff1_b, w_ff2_b) = _dsa_attention(
            q, iq, iwt, k, vt, ik, g_attn_out[l][None, :], [(w_out[l], 0), (w_ff1[l], 1), (w_ff2[l], 0)],
            batch=batch, seq=seq, tq=DSA_TQ, tk=DSA_TK)

        x1, h2 = _out_proj(pool_n, attn_n, w_out_b, x2, g_post_mix[l][None, :], gate1,
                           g_pre_ffn[l][None, :], scale2, shift2, tm=512, seq=seq)
        x2 = _ffn(h2, w_ff1_b, w_ff2_b, x1, g_post_ffn[l][None, :], gate2, tm=512, tf=1024, seq=seq)
    return x2.reshape(batch, seq, d)
```

```python
import functools

import jax
import jax.numpy as jnp
from jax import lax
from jax.experimental import pallas as pl
from jax.experimental.pallas import tpu as pltpu

POOL_WINDOWS = (2, 4, 8, 16)
POOL_GROUPS = len(POOL_WINDOWS)
HEAD_DIM = 128
KV_HEADS = 2
IDX_HEADS = 16
IDX_DIM = 128
INDEX_TOPK = 256
RMS_EPS = 1e-6

V7X_VMEM_LIMIT_BYTES = 56 * 1024 * 1024

BF16 = jnp.bfloat16
F32 = jnp.float32
MAX_BISECT_STEPS = 320
NT_DIMS = (((1,), (1,)), ((), ()))
DSA_TQ = 256
DSA_TK = 512
DSA_SUB = 256
FFN_TF = 1024
SOFTMAX_EXP2_SCALE = (HEAD_DIM ** -0.5) * 1.4426950408889634
ONES_ROWS = 16
COUNT_CHAINS = 8
BISECT_STEPS_PER_CHECK = 2
VT_ROWS = KV_HEADS * (HEAD_DIM + ONES_ROWS)


def _rms(xf, g):
    return xf * lax.rsqrt(jnp.mean(xf * xf, axis=-1, keepdims=True) + RMS_EPS) * g


def _params(sem, flags=None):
    return pltpu.CompilerParams(dimension_semantics=sem, vmem_limit_bytes=V7X_VMEM_LIMIT_BYTES, flags=flags)


def _ada_kernel(c_ref, w_ref, b_ref, o_ref):
    c = c_ref[...]
    c_act = (c * jax.nn.sigmoid(c)).astype(BF16)
    o_ref[...] = jnp.dot(c_act, w_ref[...].astype(BF16), preferred_element_type=F32) + b_ref[...]


def _ada_mod(c_pad, w_ada, b_ada, *, tn):
    mp, d = c_pad.shape
    n = w_ada.shape[1]
    return pl.pallas_call(
        _ada_kernel,
        out_shape=jax.ShapeDtypeStruct((mp, n), F32),
        grid=(n // tn,),
        in_specs=[pl.BlockSpec((mp, d), lambda j: (0, 0)),
                  pl.BlockSpec((d, tn), lambda j: (0, j)),
                  pl.BlockSpec((1, tn), lambda j: (0, j))],
        out_specs=pl.BlockSpec((mp, tn), lambda j: (0, j)),
        compiler_params=_params(("arbitrary",)),
        name="ada_mod",
    )(c_pad, w_ada, b_ada)


def _in_proj_kernel(x_ref, g_ref, sc_ref, sh_ref, w_ref, wt_ref,
                    zp_ref, q_ref, k_ref, vt_ref, iq_ref, ik_ref, iwt_ref, *, splits):
    pool_w, attn_w, kv_w, iq_w = splits
    h = (_rms(x_ref[...], g_ref[...]) * (1.0 + sc_ref[0]) + sh_ref[0]).astype(BF16)

    def proj(lo, width):
        return jnp.dot(h, w_ref[:, lo:lo + width], preferred_element_type=F32)

    off = 0
    zp_ref[...] = proj(off, pool_w).astype(BF16)
    off += pool_w
    zq = proj(off, attn_w)
    for hd in range(attn_w // HEAD_DIM):
        q_ref[hd] = (zq[:, hd * HEAD_DIM:(hd + 1) * HEAD_DIM] * SOFTMAX_EXP2_SCALE).astype(BF16)
    off += attn_w
    k_ref[...] = proj(off, kv_w).astype(BF16)
    off += 2 * kv_w
    ziq = proj(off, iq_w)
    for hd in range(iq_w // IDX_DIM):
        iq_ref[hd] = ziq[:, hd * IDX_DIM:(hd + 1) * IDX_DIM].astype(BF16)
    off += iq_w
    ik_ref[...] = proj(off, IDX_DIM).astype(BF16)
    zt = lax.dot_general(wt_ref[...], h, NT_DIMS, preferred_element_type=F32)
    vt_blk = vt_ref.shape[2]
    ones = jnp.ones((ONES_ROWS, vt_blk), BF16)
    for cb in range(vt_ref.shape[0]):
        for g in range(KV_HEADS):
            base = g * (HEAD_DIM + ONES_ROWS)
            vt_ref[cb, base:base + HEAD_DIM, :] = zt[g * HEAD_DIM:(g + 1) * HEAD_DIM,
                                                     cb * vt_blk:(cb + 1) * vt_blk].astype(BF16)
            vt_ref[cb, base + HEAD_DIM:base + HEAD_DIM + ONES_ROWS, :] = ones
    iwt_ref[...] = zt[kv_w:kv_w + IDX_HEADS]


def _in_proj(x2, g, scale, shift, w_in, w_t, *, tm, seq, splits, vt_blk):
    m, d = x2.shape
    pool_w, attn_w, kv_w, iq_w = splits
    n_heads = attn_w // HEAD_DIM
    per_b = seq // tm
    assert tm % vt_blk == 0
    row = lambda i: (i, 0)
    col = lambda i: (0, i)
    hrow = lambda i: (0, i, 0)
    hrow0 = lambda i: (i, 0, 0)
    return pl.pallas_call(
        functools.partial(_in_proj_kernel, splits=splits),
        out_shape=(jax.ShapeDtypeStruct((m, pool_w), BF16),
                   jax.ShapeDtypeStruct((n_heads, m, HEAD_DIM), BF16),
                   jax.ShapeDtypeStruct((m, kv_w), BF16),
                   jax.ShapeDtypeStruct((m // vt_blk, VT_ROWS, vt_blk), BF16),
                   jax.ShapeDtypeStruct((IDX_HEADS, m, IDX_DIM), BF16),
                   jax.ShapeDtypeStruct((m, IDX_DIM), BF16),
                   jax.ShapeDtypeStruct((IDX_HEADS, m), F32)),
        grid=(m // tm,),
        in_specs=[pl.BlockSpec((tm, d), row),
                  pl.BlockSpec((1, d), lambda i: (0, 0)),
                  pl.BlockSpec((1, 1, d), lambda i: (i // per_b, 0, 0)),
                  pl.BlockSpec((1, 1, d), lambda i: (i // per_b, 0, 0)),
                  pl.BlockSpec(w_in.shape, lambda i: (0, 0), pipeline_mode=pl.Buffered(1)),
                  pl.BlockSpec(w_t.shape, lambda i: (0, 0), pipeline_mode=pl.Buffered(1))],
        out_specs=(pl.BlockSpec((tm, pool_w), row),
                   pl.BlockSpec((n_heads, tm, HEAD_DIM), hrow),
                   pl.BlockSpec((tm, kv_w), row),
                   pl.BlockSpec((tm // vt_blk, VT_ROWS, vt_blk), hrow0),
                   pl.BlockSpec((IDX_HEADS, tm, IDX_DIM), hrow),
                   pl.BlockSpec((tm, IDX_DIM), row),
                   pl.BlockSpec((IDX_HEADS, tm), col)),
        compiler_params=_params(("arbitrary",)),
        name="in_proj",
    )(x2, g, scale, shift, w_in, w_t)


def _pool_kernel(cur_ref, prev_ref, w_ref, ps_ref, g_ref, o_ref, ext_ref, mix_ref, *, halo):
    t = pl.program_id(1)
    tt = cur_ref.shape[0]
    ch = w_ref.shape[1]
    prev = prev_ref[...].astype(F32)
    ext_ref[0:halo, :] = jnp.where(t > 0, prev, jnp.zeros_like(prev))
    ext_ref[halo:halo + tt, :] = cur_ref[...].astype(F32)
    pos = t * tt + lax.broadcasted_iota(jnp.int32, (tt, 1), 0)
    for gi, win in enumerate(POOL_WINDOWS):
        cols = slice(gi * ch, (gi + 1) * ch)
        total = ext_ref[halo:halo + tt, cols]
        for j in range(1, win):
            total = total + ext_ref[halo - j:halo - j + tt, cols]
        inv_cnt = 1.0 / jnp.minimum(pos + 1, win).astype(F32)
        pooled = total * inv_cnt - ext_ref[halo:halo + tt, cols]
        mixed = jnp.dot(pooled.astype(BF16), w_ref[gi], preferred_element_type=F32)
        mix_ref[:, cols] = mixed * ps_ref[:, cols]
    o_ref[...] = _rms(mix_ref[...], g_ref[...]).astype(BF16)


def _pool_mixer(z_pool, w_pool, pool_scale, g_pool, *, batch, seq, tt):
    m, pw = z_pool.shape
    halo = max(POOL_WINDOWS)
    per_b = seq // tt
    hb = tt // halo
    return pl.pallas_call(
        functools.partial(_pool_kernel, halo=halo),
        out_shape=jax.ShapeDtypeStruct((m, pw), BF16),
        grid=(batch, per_b),
        in_specs=[pl.BlockSpec((tt, pw), lambda b, t: (b * per_b + t, 0)),
                  pl.BlockSpec((halo, pw), lambda b, t: (jnp.maximum((b * per_b + t) * hb - 1, 0), 0)),
                  pl.BlockSpec(w_pool.shape, lambda b, t: (0, 0, 0)),
                  pl.BlockSpec((1, pw), lambda b, t: (0, 0)),
                  pl.BlockSpec((1, pw), lambda b, t: (0, 0))],
        out_specs=pl.BlockSpec((tt, pw), lambda b, t: (b * per_b + t, 0)),
        scratch_shapes=[pltpu.VMEM((tt + halo, pw), F32), pltpu.VMEM((tt, pw), F32)],
        compiler_params=_params(("arbitrary", "arbitrary")),
        name="pool_mixer",
    )(z_pool, z_pool, w_pool, pool_scale, g_pool)


def _dsa_kernel(q_ref, iq_ref, iwt_ref, k_ref, vt_ref, ik_ref, g_ref, *rest, tq, tk, topk, n_heads, n_cast):
    cast_in, (o_ref,), cast_out = rest[:n_cast], rest[n_cast:n_cast + 1], rest[n_cast + 1:2 * n_cast + 1]
    sc_ref, bias_ref, acc_ref, m_ref, l_ref, s_ref = rest[2 * n_cast + 1:]
    for src_ref, dst_ref in zip(cast_in, cast_out):
        dst_ref[...] = src_ref[...].astype(BF16).reshape(dst_ref.shape)
    qi = pl.program_id(1)
    t0 = qi * tq
    nkv = lax.div(t0 + tq + tk - 1, tk)
    q_per_kv = n_heads // KV_HEADS
    idx_scale = (IDX_HEADS ** -0.5) * (IDX_DIM ** -0.5)
    q_pos = t0 + lax.broadcasted_iota(jnp.int32, (tk, tq), 1)
    k_off = lax.broadcasted_iota(jnp.int32, (tk, tq), 0)
    w_idx = iwt_ref[...] * idx_scale

    sub = sc_ref.shape[1]
    sub_per_blk = tk // sub
    n_sub = lax.div(t0 + tq + sub - 1, sub)

    def key_rows(kb):
        return pl.ds(pl.multiple_of(kb * tk, tk), tk)

    def score_block(kb, carry):
        lo, hi = carry
        ik = ik_ref[key_rows(kb), :]
        acc = jnp.zeros((tk, tq), F32)
        for h in range(IDX_HEADS):
            y = lax.dot_general(ik, iq_ref[h], NT_DIMS, preferred_element_type=F32)
            acc = acc + w_idx[h:h + 1, :] * jnp.maximum(y, 0.0)
        causal = (kb * tk + k_off) <= q_pos
        masked = jnp.where(causal, acc, -jnp.inf)
        for c in range(sub_per_blk):
            sc_ref[kb * sub_per_blk + c] = masked[c * sub:(c + 1) * sub]
        lo = jnp.minimum(lo, jnp.min(jnp.where(causal, acc, jnp.inf), axis=0, keepdims=True))
        hi = jnp.maximum(hi, jnp.max(jnp.where(causal, acc, -jnp.inf), axis=0, keepdims=True))
        return lo, hi

    lo0, hi0 = lax.fori_loop(0, nkv, score_block,
                             (jnp.full((1, tq), jnp.inf, F32), jnp.full((1, tq), -jnp.inf, F32)))

    def count_ge(cand):
        def body(sb, part):
            hit = jnp.where(sc_ref[sb] >= cand, 1.0, 0.0)
            return part + jnp.sum(hit.reshape(COUNT_CHAINS, sub // COUNT_CHAINS, tq), axis=0)
        part = lax.fori_loop(0, n_sub, body, jnp.zeros((sub // COUNT_CHAINS, tq), F32))
        return jnp.sum(part, axis=0, keepdims=True)

    few_keys = (t0 + lax.broadcasted_iota(jnp.int32, (1, tq), 1)) < topk

    def bisect_cond(state):
        it, _, _, done = state
        return (it < MAX_BISECT_STEPS) & (jnp.min(done) < 0.5)

    def bisect_once(lo, hi, done):
        mid = 0.5 * lo + 0.5 * hi
        cnt = count_ge(mid)
        ge = cnt >= float(topk)
        move = (done < 0.5) & (mid > lo) & (mid < hi)
        new_lo = jnp.where(move & ge, mid, lo)
        new_hi = jnp.where(move & ~ge, mid, hi)
        new_done = jnp.where(move & (cnt != float(topk)), 0.0, 1.0)
        return new_lo, new_hi, new_done

    def bisect_step(state):
        it, lo, hi, done = state
        for _ in range(BISECT_STEPS_PER_CHECK):
            lo, hi, done = bisect_once(lo, hi, done)
        return it + BISECT_STEPS_PER_CHECK, lo, hi, done

    _, thr, _, _ = lax.while_loop(
        bisect_cond, bisect_step,
        (jnp.int32(0), lo0, hi0, jnp.where(few_keys, 1.0, 0.0)))

    def bias_block(sb, carry):
        bias_ref[sb] = jnp.where(sc_ref[sb] >= thr, 0.0, -jnp.inf)
        return carry

    lax.fori_loop(0, nkv * sub_per_blk, bias_block, 0)

    gq = q_per_kv * tq
    m_ref[...] = jnp.full(m_ref.shape, -jnp.inf, F32)
    l_ref[...] = jnp.zeros(l_ref.shape, F32)
    acc_ref[...] = jnp.zeros(acc_ref.shape, F32)

    def attn_block(kb, carry):
        rows = key_rows(kb)
        bias = jnp.concatenate([bias_ref[kb * sub_per_blk + c] for c in range(sub_per_blk)], axis=0)
        bias_g = jnp.concatenate([bias] * q_per_kv, axis=1)
        for g in range(KV_HEADS):
            kblk = k_ref[rows, g * HEAD_DIM:(g + 1) * HEAD_DIM]
            q_g = q_ref[g * q_per_kv:(g + 1) * q_per_kv].reshape(gq, HEAD_DIM)
            s_ref[g] = lax.dot_general(kblk, q_g, NT_DIMS, preferred_element_type=F32)
        for g in range(KV_HEADS):
            v_aug = vt_ref[kb, g * (HEAD_DIM + ONES_ROWS):(g + 1) * (HEAD_DIM + ONES_ROWS), :]
            s = s_ref[g] + bias_g
            m_run = m_ref[g:g + 1, :]
            m_new = jnp.maximum(m_run, jnp.max(s, axis=0, keepdims=True))
            m_safe = jnp.where(m_new == -jnp.inf, 0.0, m_new)
            alpha = jnp.exp2(m_run - m_safe)
            p = jnp.exp2(s - m_safe).astype(BF16)
            pv = jnp.dot(v_aug, p, preferred_element_type=F32)
            arows = slice(g * HEAD_DIM, (g + 1) * HEAD_DIM)
            acc_ref[arows, :] = alpha * acc_ref[arows, :] + pv[:HEAD_DIM]
            l_ref[g:g + 1, :] = alpha * l_ref[g:g + 1, :] + pv[HEAD_DIM:HEAD_DIM + 1]
            m_ref[g:g + 1, :] = m_new
        return carry

    lax.fori_loop(0, nkv, attn_block, 0)

    def head_view(hd):
        g, j = divmod(hd, q_per_kv)
        return g, slice(g * HEAD_DIM, (g + 1) * HEAD_DIM), slice(j * tq, (j + 1) * tq)

    ssq = jnp.zeros((1, tq), F32)
    for hd in range(n_heads):
        g, arows, lanes = head_view(hd)
        o_h = acc_ref[arows, lanes] / l_ref[g:g + 1, lanes]
        acc_ref[arows, lanes] = o_h
        ssq = ssq + jnp.sum(o_h * o_h, axis=0, keepdims=True)
    inv_rms = lax.rsqrt(ssq / float(n_heads * HEAD_DIM) + RMS_EPS)
    for hd in range(n_heads):
        _, arows, lanes = head_view(hd)
        o_t = (acc_ref[arows, lanes] * inv_rms).T
        cols = slice(hd * HEAD_DIM, (hd + 1) * HEAD_DIM)
        o_ref[:, cols] = (o_t * g_ref[:, cols]).astype(BF16)


def _dsa_attention(q, iq, iwt, k, vt, ik, g_attn, cast_weights, *, batch, seq, tq, tk):
    n_heads, m, _ = q.shape
    attn_w = n_heads * HEAD_DIM
    kv_w = k.shape[1]
    per_b = seq // tq
    n_steps = batch * per_b
    topk = min(INDEX_TOPK, seq // 4)
    gq = (n_heads // KV_HEADS) * tq
    qmap = lambda b, i: (0, b * per_b + i, 0)
    cast_in_specs, cast_out_specs, cast_shapes = [], [], []
    for w, axis, col_tile in cast_weights:
        slab = w.shape[axis] // n_steps
        assert w.shape[axis] % n_steps == 0 and slab % (16 if axis == 0 else 128) == 0
        blk = list(w.shape)
        blk[axis] = slab
        imap = (lambda b, i: (b * per_b + i, 0)) if axis == 0 else (lambda b, i: (0, b * per_b + i))
        cast_in_specs.append(pl.BlockSpec(tuple(blk), imap))
        if col_tile is None:
            cast_out_specs.append(pl.BlockSpec(tuple(blk), imap))
            cast_shapes.append(jax.ShapeDtypeStruct(w.shape, BF16))
        else:
            assert axis == 1 and col_tile % slab == 0 and w.shape[1] % col_tile == 0
            r = col_tile // slab
            cast_out_specs.append(pl.BlockSpec(
                (1, w.shape[0], slab), lambda b, i, r=r: ((b * per_b + i) // r, 0, (b * per_b + i) % r)))
            cast_shapes.append(jax.ShapeDtypeStruct((w.shape[1] // col_tile, w.shape[0], col_tile), BF16))
    n_cast = len(cast_weights)
    outs = pl.pallas_call(
        functools.partial(_dsa_kernel, tq=tq, tk=tk, topk=topk, n_heads=n_heads, n_cast=n_cast),
        out_shape=[jax.ShapeDtypeStruct((m, attn_w), BF16)] + cast_shapes,
        grid=(batch, per_b),
        in_specs=[pl.BlockSpec((n_heads, tq, HEAD_DIM), qmap),
                  pl.BlockSpec((IDX_HEADS, tq, IDX_DIM), qmap),
                  pl.BlockSpec((IDX_HEADS, tq), lambda b, i: (0, b * per_b + i)),
                  pl.BlockSpec((seq, kv_w), lambda b, i: (b, 0)),
                  pl.BlockSpec((seq // tk, VT_ROWS, tk), lambda b, i: (b, 0, 0)),
                  pl.BlockSpec((seq, IDX_DIM), lambda b, i: (b, 0)),
                  pl.BlockSpec((1, attn_w), lambda b, i: (0, 0))] + cast_in_specs,
        out_specs=[pl.BlockSpec((tq, attn_w), lambda b, i: (b * per_b + i, 0))] + cast_out_specs,
        scratch_shapes=[pltpu.VMEM((seq // DSA_SUB, DSA_SUB, tq), F32),
                        pltpu.VMEM((seq // DSA_SUB, DSA_SUB, tq), F32),
                        pltpu.VMEM((KV_HEADS * HEAD_DIM, gq), F32),
                        pltpu.VMEM((8, gq), F32),
                        pltpu.VMEM((8, gq), F32),
                        pltpu.VMEM((KV_HEADS, tk, gq), F32)],
        compiler_params=_params(("arbitrary", "arbitrary")),
        name="dsa_attn",
    )(q, iq, iwt, k, vt, ik, g_attn, *[w for w, _, _ in cast_weights])
    return outs[0], outs[1:]


def _out_proj_kernel(pool_ref, attn_ref, w_ref, x_ref, gpost_ref, gate_ref, gpre_ref, sc_ref, sh_ref,
                     x1_ref, h2_ref):
    pw = pool_ref.shape[1]
    mix = jnp.dot(pool_ref[...], w_ref[0:pw, :], preferred_element_type=F32)
    mix = mix + jnp.dot(attn_ref[...], w_ref[pw:, :], preferred_element_type=F32)
    x1 = x_ref[...] + gate_ref[0] * _rms(mix, gpost_ref[...])
    x1_ref[...] = x1
    h2_ref[...] = (_rms(x1, gpre_ref[...]) * (1.0 + sc_ref[0]) + sh_ref[0]).astype(BF16)


def _out_proj(pool_n, attn_n, w_out, x2, g_post, gate, g_pre, scale, shift, *, tm, seq):
    m, d = x2.shape
    pw, aw = pool_n.shape[1], attn_n.shape[1]
    per_b = seq // tm
    row = lambda i: (i, 0)
    vec = lambda i: (0, 0)
    bvec = lambda i: (i // per_b, 0, 0)
    return pl.pallas_call(
        _out_proj_kernel,
        out_shape=(jax.ShapeDtypeStruct((m, d), F32), jax.ShapeDtypeStruct((m, d), BF16)),
        grid=(m // tm,),
        in_specs=[pl.BlockSpec((tm, pw), row),
                  pl.BlockSpec((tm, aw), row),
                  pl.BlockSpec(w_out.shape, vec, pipeline_mode=pl.Buffered(1)),
                  pl.BlockSpec((tm, d), row),
                  pl.BlockSpec((1, d), vec),
                  pl.BlockSpec((1, 1, d), bvec),
                  pl.BlockSpec((1, d), vec),
                  pl.BlockSpec((1, 1, d), bvec),
                  pl.BlockSpec((1, 1, d), bvec)],
        out_specs=(pl.BlockSpec((tm, d), row), pl.BlockSpec((tm, d), row)),
        compiler_params=_params(("arbitrary",)),
        name="out_proj",
    )(pool_n, attn_n, w_out, x2, g_post, gate, g_pre, scale, shift)


def _ffn_kernel(h_ref, w1_ref, w2_ref, x_ref, g_ref, gate_ref, o_ref, acc_ref):
    j = pl.program_id(1)
    a = jnp.dot(h_ref[...], w1_ref[0], preferred_element_type=F32)
    a = jnp.square(jnp.maximum(a, 0.0)).astype(BF16)
    part = jnp.dot(a, w2_ref[...], preferred_element_type=F32)
    acc_ref[...] = jnp.where(j == 0, part, acc_ref[...] + part)

    @pl.when(j == pl.num_programs(1) - 1)
    def _():
        o_ref[...] = x_ref[...] + gate_ref[0] * _rms(acc_ref[...], g_ref[...])


def _ffn(h2, w1, w2, x1, g_post, gate, *, tm, seq):
    m, d = x1.shape
    n_f, _, tf = w1.shape
    per_b = seq // tm
    return pl.pallas_call(
        _ffn_kernel,
        out_shape=jax.ShapeDtypeStruct((m, d), F32),
        grid=(m // tm, n_f),
        in_specs=[pl.BlockSpec((tm, d), lambda i, j: (i, 0)),
                  pl.BlockSpec((1, d, tf), lambda i, j: (j, 0, 0)),
                  pl.BlockSpec((tf, d), lambda i, j: (j, 0)),
                  pl.BlockSpec((tm, d), lambda i, j: (i, 0)),
                  pl.BlockSpec((1, d), lambda i, j: (0, 0)),
                  pl.BlockSpec((1, 1, d), lambda i, j: (i // per_b, 0, 0))],
        out_specs=pl.BlockSpec((tm, d), lambda i, j: (i, 0)),
        scratch_shapes=[pltpu.VMEM((tm, d), F32)],
        compiler_params=_params(("arbitrary", "arbitrary")),
        name="ffn",
    )(h2, w1, w2, x1, g_post, gate)


def kernel(x, c, w_ada, b_ada, g_pre_mix, g_post_mix, g_pre_ffn, g_post_ffn, w_in, w_pool, pool_scale,
           g_pool_out, g_attn_out, w_out, w_ff1, w_ff2):
    batch, seq, d = x.shape
    depth = w_ada.shape[0]
    m = batch * seq
    pool_w = pool_scale.shape[1]
    attn_w = g_attn_out.shape[1]
    kv_w = KV_HEADS * HEAD_DIM
    iq_w = IDX_HEADS * IDX_DIM
    splits = (pool_w, attn_w, kv_w, iq_w)
    in_w = w_in.shape[2]
    assert in_w == pool_w + attn_w + 2 * kv_w + iq_w + IDX_DIM + IDX_HEADS
    v_lo = pool_w + attn_w + kv_w

    x2 = x.reshape(m, d)
    c_pad = jnp.zeros((8, d), F32).at[:batch].set(c)
    for l in range(depth):
        mod = _ada_mod(c_pad, w_ada[l], b_ada[l][None, :], tn=1024)[:batch]
        shift1, scale1, gate1, shift2, scale2, gate2 = [
            mod[:, i * d:(i + 1) * d][:, None, :] for i in range(6)]

        w_in_b = w_in[l].astype(BF16)
        w_vi = lax.optimization_barrier((w_in[l][:, v_lo:v_lo + kv_w], w_in[l][:, in_w - IDX_HEADS:]))
        w_t = jnp.concatenate(w_vi, axis=1).T.astype(BF16)
        z_pool, q, k, vt, iq, ik, iwt = _in_proj(
            x2, g_pre_mix[l][None, :], scale1, shift1, w_in_b, w_t, tm=512, seq=seq, splits=splits,
            vt_blk=DSA_TK)

        pool_n = _pool_mixer(z_pool, w_pool[l].astype(BF16), pool_scale[l][None, :], g_pool_out[l][None, :],
                             batch=batch, seq=seq, tt=512)
        attn_n, (w_out_b, w_ff1_b, w_ff2_b) = _dsa_attention(
            q, iq, iwt, k, vt, ik, g_attn_out[l][None, :],
            [(w_out[l], 0, None), (w_ff1[l], 1, FFN_TF), (w_ff2[l], 0, None)],
            batch=batch, seq=seq, tq=DSA_TQ, tk=DSA_TK)

        x1, h2 = _out_proj(pool_n, attn_n, w_out_b, x2, g_post_mix[l][None, :], gate1,
                           g_pre_ffn[l][None, :], scale2, shift2, tm=512, seq=seq)
        x2 = _ffn(h2, w_ff1_b, w_ff2_b, x1, g_post_ffn[l][None, :], gate2, tm=512, seq=seq)
    return x2.reshape(batch, seq, d)
```

```python
import functools

import jax
import jax.numpy as jnp
from jax import lax
from jax.experimental import pallas as pl
from jax.experimental.pallas import tpu as pltpu

POOL_WINDOWS = (2, 4, 8, 16)
POOL_GROUPS = len(POOL_WINDOWS)
HEAD_DIM = 128
KV_HEADS = 2
IDX_HEADS = 16
IDX_DIM = 128
INDEX_TOPK = 256
RMS_EPS = 1e-6

V7X_VMEM_LIMIT_BYTES = 56 * 1024 * 1024

BF16 = jnp.bfloat16
F32 = jnp.float32
MAX_BISECT_STEPS = 320
NT_DIMS = (((1,), (1,)), ((), ()))
DSA_TQ = 256
DSA_TK = 512
DSA_SUB = 256
FFN_TF = 1024
SOFTMAX_EXP2_SCALE = (HEAD_DIM ** -0.5) * 1.4426950408889634
ONES_ROWS = 16
COUNT_CHAINS = 8
BISECT_STEPS_PER_CHECK = 2
VT_ROWS = KV_HEADS * (HEAD_DIM + ONES_ROWS)


def _rms(xf, g):
    return xf * lax.rsqrt(jnp.mean(xf * xf, axis=-1, keepdims=True) + RMS_EPS) * g


def _params(sem, flags=None):
    return pltpu.CompilerParams(dimension_semantics=sem, vmem_limit_bytes=V7X_VMEM_LIMIT_BYTES, flags=flags)


def _ada_kernel(c_ref, w_ref, b_ref, win_ref, o_ref, winb_ref):
    c = c_ref[...]
    c_act = (c * jax.nn.sigmoid(c)).astype(BF16)
    o_ref[...] = jnp.dot(c_act, w_ref[...].astype(BF16), preferred_element_type=F32) + b_ref[...]
    winb_ref[...] = win_ref[...].astype(BF16)


def _ada_mod(c_pad, w_ada, b_ada, w_in, *, n_steps):
    mp, d = c_pad.shape
    n = w_ada.shape[1]
    tn = n // n_steps
    rows, in_w = w_in.shape
    slab = rows // n_steps
    assert n % n_steps == 0 and tn % 128 == 0 and rows % n_steps == 0 and slab % 16 == 0
    return pl.pallas_call(
        _ada_kernel,
        out_shape=(jax.ShapeDtypeStruct((mp, n), F32), jax.ShapeDtypeStruct((rows, in_w), BF16)),
        grid=(n_steps,),
        in_specs=[pl.BlockSpec((mp, d), lambda j: (0, 0)),
                  pl.BlockSpec((d, tn), lambda j: (0, j)),
                  pl.BlockSpec((1, tn), lambda j: (0, j)),
                  pl.BlockSpec((slab, in_w), lambda j: (j, 0))],
        out_specs=(pl.BlockSpec((mp, tn), lambda j: (0, j)),
                   pl.BlockSpec((slab, in_w), lambda j: (j, 0))),
        compiler_params=_params(("arbitrary",)),
        name="ada_mod",
    )(c_pad, w_ada, b_ada, w_in)


def _in_proj_kernel(x_ref, g_ref, sc_ref, sh_ref, w_ref, wt_ref,
                    zp_ref, q_ref, k_ref, vt_ref, iq_ref, ik_ref, iwt_ref, *, splits):
    pool_w, attn_w, kv_w, iq_w = splits
    h = (_rms(x_ref[...], g_ref[...]) * (1.0 + sc_ref[0]) + sh_ref[0]).astype(BF16)

    def proj(lo, width):
        return jnp.dot(h, w_ref[:, lo:lo + width], preferred_element_type=F32)

    off = 0
    zp_ref[...] = proj(off, pool_w).astype(BF16)
    off += pool_w
    zq = proj(off, attn_w)
    for hd in range(attn_w // HEAD_DIM):
        q_ref[hd] = (zq[:, hd * HEAD_DIM:(hd + 1) * HEAD_DIM] * SOFTMAX_EXP2_SCALE).astype(BF16)
    off += attn_w
    k_ref[...] = proj(off, kv_w).astype(BF16)
    off += 2 * kv_w
    ziq = proj(off, iq_w)
    for hd in range(iq_w // IDX_DIM):
        iq_ref[hd] = ziq[:, hd * IDX_DIM:(hd + 1) * IDX_DIM].astype(BF16)
    off += iq_w
    ik_ref[...] = proj(off, IDX_DIM).astype(BF16)
    zt = lax.dot_general(wt_ref[...], h, NT_DIMS, preferred_element_type=F32)
    vt_blk = vt_ref.shape[2]
    ones = jnp.ones((ONES_ROWS, vt_blk), BF16)
    for cb in range(vt_ref.shape[0]):
        for g in range(KV_HEADS):
            base = g * (HEAD_DIM + ONES_ROWS)
            vt_ref[cb, base:base + HEAD_DIM, :] = zt[g * HEAD_DIM:(g + 1) * HEAD_DIM,
                                                     cb * vt_blk:(cb + 1) * vt_blk].astype(BF16)
            vt_ref[cb, base + HEAD_DIM:base + HEAD_DIM + ONES_ROWS, :] = ones
    iwt_ref[...] = zt[kv_w:kv_w + IDX_HEADS]


def _in_proj(x2, g, scale, shift, w_in, w_t, *, tm, seq, splits, vt_blk):
    m, d = x2.shape
    pool_w, attn_w, kv_w, iq_w = splits
    n_heads = attn_w // HEAD_DIM
    per_b = seq // tm
    assert tm % vt_blk == 0
    row = lambda i: (i, 0)
    col = lambda i: (0, i)
    hrow = lambda i: (0, i, 0)
    hrow0 = lambda i: (i, 0, 0)
    return pl.pallas_call(
        functools.partial(_in_proj_kernel, splits=splits),
        out_shape=(jax.ShapeDtypeStruct((m, pool_w), BF16),
                   jax.ShapeDtypeStruct((n_heads, m, HEAD_DIM), BF16),
                   jax.ShapeDtypeStruct((m, kv_w), BF16),
                   jax.ShapeDtypeStruct((m // vt_blk, VT_ROWS, vt_blk), BF16),
                   jax.ShapeDtypeStruct((IDX_HEADS, m, IDX_DIM), BF16),
                   jax.ShapeDtypeStruct((m, IDX_DIM), BF16),
                   jax.ShapeDtypeStruct((IDX_HEADS, m), F32)),
        grid=(m // tm,),
        in_specs=[pl.BlockSpec((tm, d), row),
                  pl.BlockSpec((1, d), lambda i: (0, 0)),
                  pl.BlockSpec((1, 1, d), lambda i: (i // per_b, 0, 0)),
                  pl.BlockSpec((1, 1, d), lambda i: (i // per_b, 0, 0)),
                  pl.BlockSpec(w_in.shape, lambda i: (0, 0), pipeline_mode=pl.Buffered(1)),
                  pl.BlockSpec(w_t.shape, lambda i: (0, 0), pipeline_mode=pl.Buffered(1))],
        out_specs=(pl.BlockSpec((tm, pool_w), row),
                   pl.BlockSpec((n_heads, tm, HEAD_DIM), hrow),
                   pl.BlockSpec((tm, kv_w), row),
                   pl.BlockSpec((tm // vt_blk, VT_ROWS, vt_blk), hrow0),
                   pl.BlockSpec((IDX_HEADS, tm, IDX_DIM), hrow),
                   pl.BlockSpec((tm, IDX_DIM), row),
                   pl.BlockSpec((IDX_HEADS, tm), col)),
        compiler_params=_params(("arbitrary",)),
        name="in_proj",
    )(x2, g, scale, shift, w_in, w_t)


def _pool_kernel(cur_ref, prev_ref, w_ref, ps_ref, g_ref, o_ref, ext_ref, mix_ref, *, halo):
    t = pl.program_id(1)
    tt = cur_ref.shape[0]
    ch = w_ref.shape[1]
    prev = prev_ref[...].astype(F32)
    ext_ref[0:halo, :] = jnp.where(t > 0, prev, jnp.zeros_like(prev))
    ext_ref[halo:halo + tt, :] = cur_ref[...].astype(F32)
    pos = t * tt + lax.broadcasted_iota(jnp.int32, (tt, 1), 0)
    for gi, win in enumerate(POOL_WINDOWS):
        cols = slice(gi * ch, (gi + 1) * ch)
        total = ext_ref[halo:halo + tt, cols]
        for j in range(1, win):
            total = total + ext_ref[halo - j:halo - j + tt, cols]
        inv_cnt = 1.0 / jnp.minimum(pos + 1, win).astype(F32)
        pooled = total * inv_cnt - ext_ref[halo:halo + tt, cols]
        mixed = jnp.dot(pooled.astype(BF16), w_ref[gi], preferred_element_type=F32)
        mix_ref[:, cols] = mixed * ps_ref[:, cols]
    o_ref[...] = _rms(mix_ref[...], g_ref[...]).astype(BF16)


def _pool_mixer(z_pool, w_pool, pool_scale, g_pool, *, batch, seq, tt):
    m, pw = z_pool.shape
    halo = max(POOL_WINDOWS)
    per_b = seq // tt
    hb = tt // halo
    return pl.pallas_call(
        functools.partial(_pool_kernel, halo=halo),
        out_shape=jax.ShapeDtypeStruct((m, pw), BF16),
        grid=(batch, per_b),
        in_specs=[pl.BlockSpec((tt, pw), lambda b, t: (b * per_b + t, 0)),
                  pl.BlockSpec((halo, pw), lambda b, t: (jnp.maximum((b * per_b + t) * hb - 1, 0), 0)),
                  pl.BlockSpec(w_pool.shape, lambda b, t: (0, 0, 0)),
                  pl.BlockSpec((1, pw), lambda b, t: (0, 0)),
                  pl.BlockSpec((1, pw), lambda b, t: (0, 0))],
        out_specs=pl.BlockSpec((tt, pw), lambda b, t: (b * per_b + t, 0)),
        scratch_shapes=[pltpu.VMEM((tt + halo, pw), F32), pltpu.VMEM((tt, pw), F32)],
        compiler_params=_params(("arbitrary", "arbitrary")),
        name="pool_mixer",
    )(z_pool, z_pool, w_pool, pool_scale, g_pool)


def _dsa_kernel(q_ref, iq_ref, iwt_ref, k_ref, vt_ref, ik_ref, g_ref, *rest, tq, tk, topk, n_heads, n_cast):
    cast_in, (o_ref,), cast_out = rest[:n_cast], rest[n_cast:n_cast + 1], rest[n_cast + 1:2 * n_cast + 1]
    sc_ref, bias_ref, acc_ref, m_ref, l_ref, s_ref = rest[2 * n_cast + 1:]
    for src_ref, dst_ref in zip(cast_in, cast_out):
        dst_ref[...] = src_ref[...].astype(BF16).reshape(dst_ref.shape)
    qi = pl.program_id(1)
    t0 = qi * tq
    nkv = lax.div(t0 + tq + tk - 1, tk)
    q_per_kv = n_heads // KV_HEADS
    idx_scale = (IDX_HEADS ** -0.5) * (IDX_DIM ** -0.5)
    q_pos = t0 + lax.broadcasted_iota(jnp.int32, (tk, tq), 1)
    k_off = lax.broadcasted_iota(jnp.int32, (tk, tq), 0)
    w_idx = iwt_ref[...] * idx_scale

    sub = sc_ref.shape[1]
    sub_per_blk = tk // sub
    n_sub = lax.div(t0 + tq + sub - 1, sub)

    def key_rows(kb):
        return pl.ds(pl.multiple_of(kb * tk, tk), tk)

    def score_block(kb, carry):
        lo, hi = carry
        ik = ik_ref[key_rows(kb), :]
        acc = jnp.zeros((tk, tq), F32)
        for h in range(IDX_HEADS):
            y = lax.dot_general(ik, iq_ref[h], NT_DIMS, preferred_element_type=F32)
            acc = acc + w_idx[h:h + 1, :] * jnp.maximum(y, 0.0)
        causal = (kb * tk + k_off) <= q_pos
        masked = jnp.where(causal, acc, -jnp.inf)
        for c in range(sub_per_blk):
            sc_ref[kb * sub_per_blk + c] = masked[c * sub:(c + 1) * sub]
        lo = jnp.minimum(lo, jnp.min(jnp.where(causal, acc, jnp.inf), axis=0, keepdims=True))
        hi = jnp.maximum(hi, jnp.max(jnp.where(causal, acc, -jnp.inf), axis=0, keepdims=True))
        return lo, hi

    lo0, hi0 = lax.fori_loop(0, nkv, score_block,
                             (jnp.full((1, tq), jnp.inf, F32), jnp.full((1, tq), -jnp.inf, F32)))

    def count_ge(cand):
        def body(sb, part):
            hit = jnp.where(sc_ref[sb] >= cand, 1.0, 0.0)
            return part + jnp.sum(hit.reshape(COUNT_CHAINS, sub // COUNT_CHAINS, tq), axis=0)
        part = lax.fori_loop(0, n_sub, body, jnp.zeros((sub // COUNT_CHAINS, tq), F32))
        return jnp.sum(part, axis=0, keepdims=True)

    few_keys = (t0 + lax.broadcasted_iota(jnp.int32, (1, tq), 1)) < topk

    def bisect_cond(state):
        it, _, _, done = state
        return (it < MAX_BISECT_STEPS) & (jnp.min(done) < 0.5)

    def bisect_once(lo, hi, done):
        mid = 0.5 * lo + 0.5 * hi
        cnt = count_ge(mid)
        ge = cnt >= float(topk)
        move = (done < 0.5) & (mid > lo) & (mid < hi)
        new_lo = jnp.where(move & ge, mid, lo)
        new_hi = jnp.where(move & ~ge, mid, hi)
        new_done = jnp.where(move & (cnt != float(topk)), 0.0, 1.0)
        return new_lo, new_hi, new_done

    def bisect_step(state):
        it, lo, hi, done = state
        for _ in range(BISECT_STEPS_PER_CHECK):
            lo, hi, done = bisect_once(lo, hi, done)
        return it + BISECT_STEPS_PER_CHECK, lo, hi, done

    _, thr, _, _ = lax.while_loop(
        bisect_cond, bisect_step,
        (jnp.int32(0), lo0, hi0, jnp.where(few_keys, 1.0, 0.0)))

    def bias_block(sb, carry):
        bias_ref[sb] = jnp.where(sc_ref[sb] >= thr, 0.0, -jnp.inf)
        return carry

    lax.fori_loop(0, nkv * sub_per_blk, bias_block, 0)

    gq = q_per_kv * tq
    m_ref[...] = jnp.full(m_ref.shape, -jnp.inf, F32)
    l_ref[...] = jnp.zeros(l_ref.shape, F32)
    acc_ref[...] = jnp.zeros(acc_ref.shape, F32)

    def attn_block(kb, carry):
        rows = key_rows(kb)
        bias = jnp.concatenate([bias_ref[kb * sub_per_blk + c] for c in range(sub_per_blk)], axis=0)
        bias_g = jnp.concatenate([bias] * q_per_kv, axis=1)
        for g in range(KV_HEADS):
            kblk = k_ref[rows, g * HEAD_DIM:(g + 1) * HEAD_DIM]
            q_g = q_ref[g * q_per_kv:(g + 1) * q_per_kv].reshape(gq, HEAD_DIM)
            s_ref[g] = lax.dot_general(kblk, q_g, NT_DIMS, preferred_element_type=F32)
        for g in range(KV_HEADS):
            v_aug = vt_ref[kb, g * (HEAD_DIM + ONES_ROWS):(g + 1) * (HEAD_DIM + ONES_ROWS), :]
            s = s_ref[g] + bias_g
            m_run = m_ref[g:g + 1, :]
            m_new = jnp.maximum(m_run, jnp.max(s, axis=0, keepdims=True))
            m_safe = jnp.where(m_new == -jnp.inf, 0.0, m_new)
            alpha = jnp.exp2(m_run - m_safe)
            p = jnp.exp2(s - m_safe).astype(BF16)
            pv = jnp.dot(v_aug, p, preferred_element_type=F32)
            arows = slice(g * HEAD_DIM, (g + 1) * HEAD_DIM)
            acc_ref[arows, :] = alpha * acc_ref[arows, :] + pv[:HEAD_DIM]
            l_ref[g:g + 1, :] = alpha * l_ref[g:g + 1, :] + pv[HEAD_DIM:HEAD_DIM + 1]
            m_ref[g:g + 1, :] = m_new
        return carry

    lax.fori_loop(0, nkv, attn_block, 0)

    def head_view(hd):
        g, j = divmod(hd, q_per_kv)
        return g, slice(g * HEAD_DIM, (g + 1) * HEAD_DIM), slice(j * tq, (j + 1) * tq)

    ssq = jnp.zeros((1, tq), F32)
    for hd in range(n_heads):
        g, arows, lanes = head_view(hd)
        o_h = acc_ref[arows, lanes] / l_ref[g:g + 1, lanes]
        acc_ref[arows, lanes] = o_h
        ssq = ssq + jnp.sum(o_h * o_h, axis=0, keepdims=True)
    inv_rms = lax.rsqrt(ssq / float(n_heads * HEAD_DIM) + RMS_EPS)
    for hd in range(n_heads):
        _, arows, lanes = head_view(hd)
        o_t = (acc_ref[arows, lanes] * inv_rms).T
        cols = slice(hd * HEAD_DIM, (hd + 1) * HEAD_DIM)
        o_ref[:, cols] = (o_t * g_ref[:, cols]).astype(BF16)


def _dsa_attention(q, iq, iwt, k, vt, ik, g_attn, cast_weights, *, batch, seq, tq, tk):
    n_heads, m, _ = q.shape
    attn_w = n_heads * HEAD_DIM
    kv_w = k.shape[1]
    per_b = seq // tq
    n_steps = batch * per_b
    topk = min(INDEX_TOPK, seq // 4)
    gq = (n_heads // KV_HEADS) * tq
    qmap = lambda b, i: (0, b * per_b + i, 0)
    cast_in_specs, cast_out_specs, cast_shapes = [], [], []
    for w, axis, col_tile in cast_weights:
        slab = w.shape[axis] // n_steps
        assert w.shape[axis] % n_steps == 0 and slab % (16 if axis == 0 else 128) == 0
        blk = list(w.shape)
        blk[axis] = slab
        imap = (lambda b, i: (b * per_b + i, 0)) if axis == 0 else (lambda b, i: (0, b * per_b + i))
        cast_in_specs.append(pl.BlockSpec(tuple(blk), imap))
        if col_tile is None:
            cast_out_specs.append(pl.BlockSpec(tuple(blk), imap))
            cast_shapes.append(jax.ShapeDtypeStruct(w.shape, BF16))
        else:
            assert axis == 1 and col_tile % slab == 0 and w.shape[1] % col_tile == 0
            r = col_tile // slab
            cast_out_specs.append(pl.BlockSpec(
                (1, w.shape[0], slab), lambda b, i, r=r: ((b * per_b + i) // r, 0, (b * per_b + i) % r)))
            cast_shapes.append(jax.ShapeDtypeStruct((w.shape[1] // col_tile, w.shape[0], col_tile), BF16))
    n_cast = len(cast_weights)
    outs = pl.pallas_call(
        functools.partial(_dsa_kernel, tq=tq, tk=tk, topk=topk, n_heads=n_heads, n_cast=n_cast),
        out_shape=[jax.ShapeDtypeStruct((m, attn_w), BF16)] + cast_shapes,
        grid=(batch, per_b),
        in_specs=[pl.BlockSpec((n_heads, tq, HEAD_DIM), qmap),
                  pl.BlockSpec((IDX_HEADS, tq, IDX_DIM), qmap),
                  pl.BlockSpec((IDX_HEADS, tq), lambda b, i: (0, b * per_b + i)),
                  pl.BlockSpec((seq, kv_w), lambda b, i: (b, 0)),
                  pl.BlockSpec((seq // tk, VT_ROWS, tk), lambda b, i: (b, 0, 0)),
                  pl.BlockSpec((seq, IDX_DIM), lambda b, i: (b, 0)),
                  pl.BlockSpec((1, attn_w), lambda b, i: (0, 0))] + cast_in_specs,
        out_specs=[pl.BlockSpec((tq, attn_w), lambda b, i: (b * per_b + i, 0))] + cast_out_specs,
        scratch_shapes=[pltpu.VMEM((seq // DSA_SUB, DSA_SUB, tq), F32),
                        pltpu.VMEM((seq // DSA_SUB, DSA_SUB, tq), F32),
                        pltpu.VMEM((KV_HEADS * HEAD_DIM, gq), F32),
                        pltpu.VMEM((8, gq), F32),
                        pltpu.VMEM((8, gq), F32),
                        pltpu.VMEM((KV_HEADS, tk, gq), F32)],
        compiler_params=_params(("arbitrary", "arbitrary")),
        name="dsa_attn",
    )(q, iq, iwt, k, vt, ik, g_attn, *[w for w, _, _ in cast_weights])
    return outs[0], outs[1:]


def _out_proj_kernel(pool_ref, attn_ref, w_ref, x_ref, gpost_ref, gate_ref, gpre_ref, sc_ref, sh_ref,
                     x1_ref, h2_ref):
    pw = pool_ref.shape[1]
    mix = jnp.dot(pool_ref[...], w_ref[0:pw, :], preferred_element_type=F32)
    mix = mix + jnp.dot(attn_ref[...], w_ref[pw:, :], preferred_element_type=F32)
    x1 = x_ref[...] + gate_ref[0] * _rms(mix, gpost_ref[...])
    x1_ref[...] = x1
    h2_ref[...] = (_rms(x1, gpre_ref[...]) * (1.0 + sc_ref[0]) + sh_ref[0]).astype(BF16)


def _out_proj(pool_n, attn_n, w_out, x2, g_post, gate, g_pre, scale, shift, *, tm, seq):
    m, d = x2.shape
    pw, aw = pool_n.shape[1], attn_n.shape[1]
    per_b = seq // tm
    row = lambda i: (i, 0)
    vec = lambda i: (0, 0)
    bvec = lambda i: (i // per_b, 0, 0)
    return pl.pallas_call(
        _out_proj_kernel,
        out_shape=(jax.ShapeDtypeStruct((m, d), F32), jax.ShapeDtypeStruct((m, d), BF16)),
        grid=(m // tm,),
        in_specs=[pl.BlockSpec((tm, pw), row),
                  pl.BlockSpec((tm, aw), row),
                  pl.BlockSpec(w_out.shape, vec, pipeline_mode=pl.Buffered(1)),
                  pl.BlockSpec((tm, d), row),
                  pl.BlockSpec((1, d), vec),
                  pl.BlockSpec((1, 1, d), bvec),
                  pl.BlockSpec((1, d), vec),
                  pl.BlockSpec((1, 1, d), bvec),
                  pl.BlockSpec((1, 1, d), bvec)],
        out_specs=(pl.BlockSpec((tm, d), row), pl.BlockSpec((tm, d), row)),
        compiler_params=_params(("arbitrary",)),
        name="out_proj",
    )(pool_n, attn_n, w_out, x2, g_post, gate, g_pre, scale, shift)


def _ffn_kernel(h_ref, w1_ref, w2_ref, x_ref, g_ref, gate_ref, o_ref, acc_ref):
    j = pl.program_id(1)

    @pl.when(j == 0)
    def _():
        acc_ref[...] = jnp.zeros_like(acc_ref)

    a = jnp.dot(h_ref[...], w1_ref[0], preferred_element_type=F32)
    a = jnp.square(jnp.maximum(a, 0.0)).astype(BF16)
    acc_ref[...] += jnp.dot(a, w2_ref[...], preferred_element_type=F32)

    @pl.when(j == pl.num_programs(1) - 1)
    def _():
        o_ref[...] = x_ref[...] + gate_ref[0] * _rms(acc_ref[...], g_ref[...])


def _ffn(h2, w1, w2, x1, g_post, gate, *, tm, seq):
    m, d = x1.shape
    n_f, _, tf = w1.shape
    per_b = seq // tm
    return pl.pallas_call(
        _ffn_kernel,
        out_shape=jax.ShapeDtypeStruct((m, d), F32),
        grid=(m // tm, n_f),
        in_specs=[pl.BlockSpec((tm, d), lambda i, j: (i, 0)),
                  pl.BlockSpec((1, d, tf), lambda i, j: (j, 0, 0)),
                  pl.BlockSpec((tf, d), lambda i, j: (j, 0)),
                  pl.BlockSpec((tm, d), lambda i, j: (i, 0)),
                  pl.BlockSpec((1, d), lambda i, j: (0, 0)),
                  pl.BlockSpec((1, 1, d), lambda i, j: (i // per_b, 0, 0))],
        out_specs=pl.BlockSpec((tm, d), lambda i, j: (i, 0)),
        scratch_shapes=[pltpu.VMEM((tm, d), F32)],
        compiler_params=_params(("arbitrary", "arbitrary")),
        name="ffn",
    )(h2, w1, w2, x1, g_post, gate)


def kernel(x, c, w_ada, b_ada, g_pre_mix, g_post_mix, g_pre_ffn, g_post_ffn, w_in, w_pool, pool_scale,
           g_pool_out, g_attn_out, w_out, w_ff1, w_ff2):
    batch, seq, d = x.shape
    depth = w_ada.shape[0]
    m = batch * seq
    pool_w = pool_scale.shape[1]
    attn_w = g_attn_out.shape[1]
    kv_w = KV_HEADS * HEAD_DIM
    iq_w = IDX_HEADS * IDX_DIM
    splits = (pool_w, attn_w, kv_w, iq_w)
    in_w = w_in.shape[2]
    assert in_w == pool_w + attn_w + 2 * kv_w + iq_w + IDX_DIM + IDX_HEADS
    v_lo = pool_w + attn_w + kv_w

    x2 = x.reshape(m, d)
    c_pad = jnp.zeros((8, d), F32).at[:batch].set(c)
    for l in range(depth):
        mod, w_in_b = _ada_mod(c_pad, w_ada[l], b_ada[l][None, :], w_in[l], n_steps=16)
        mod = mod[:batch]
        shift1, scale1, gate1, shift2, scale2, gate2 = [
            mod[:, i * d:(i + 1) * d][:, None, :] for i in range(6)]

        w_t = jnp.concatenate([w_in[l][:, v_lo:v_lo + kv_w], w_in[l][:, in_w - IDX_HEADS:]], axis=1).T.astype(BF16)
        z_pool, q, k, vt, iq, ik, iwt = _in_proj(
            x2, g_pre_mix[l][None, :], scale1, shift1, w_in_b, w_t, tm=512, seq=seq, splits=splits,
            vt_blk=DSA_TK)

        pool_n = _pool_mixer(z_pool, w_pool[l].astype(BF16), pool_scale[l][None, :], g_pool_out[l][None, :],
                             batch=batch, seq=seq, tt=512)
        attn_n, (w_out_b, w_ff1_b, w_ff2_b) = _dsa_attention(
            q, iq, iwt, k, vt, ik, g_attn_out[l][None, :],
            [(w_out[l], 0, None), (w_ff1[l], 1, FFN_TF), (w_ff2[l], 0, None)],
            batch=batch, seq=seq, tq=DSA_TQ, tk=DSA_TK)

        x1, h2 = _out_proj(pool_n, attn_n, w_out_b, x2, g_post_mix[l][None, :], gate1,
                           g_pre_ffn[l][None, :], scale2, shift2, tm=512, seq=seq)
        x2 = _ffn(h2, w_ff1_b, w_ff2_b, x1, g_post_ffn[l][None, :], gate2, tm=512, seq=seq)
    return x2.reshape(batch, seq, d)
```

```python
import functools

import jax
import jax.numpy as jnp
from jax import lax
from jax.experimental import pallas as pl
from jax.experimental.pallas import tpu as pltpu

POOL_WINDOWS = (2, 4, 8, 16)
POOL_GROUPS = len(POOL_WINDOWS)
HEAD_DIM = 128
KV_HEADS = 2
IDX_HEADS = 16
IDX_DIM = 128
INDEX_TOPK = 256
RMS_EPS = 1e-6

V7X_LANES = 128
V7X_VMEM_LIMIT_BYTES = 56 * 1024 * 1024

BF16 = jnp.bfloat16
F32 = jnp.float32
MAX_BISECT_STEPS = 320
NT_DIMS = (((1,), (1,)), ((), ()))
DSA_TQ = 256
DSA_TK = 512
DSA_SUB = 256
FFN_TF = 1024
SOFTMAX_EXP2_SCALE = (HEAD_DIM ** -0.5) * 1.4426950408889634
ONES_ROWS = 16
COUNT_CHAINS = 8
BISECT_STEPS_PER_CHECK = 2
VT_ROWS = KV_HEADS * (HEAD_DIM + ONES_ROWS)


def _rms(xf, g):
    return xf * lax.rsqrt(jnp.mean(xf * xf, axis=-1, keepdims=True) + RMS_EPS) * g


def _params(sem, flags=None):
    return pltpu.CompilerParams(dimension_semantics=sem, vmem_limit_bytes=V7X_VMEM_LIMIT_BYTES, flags=flags)


def _ada_kernel(c_ref, w_ref, b_ref, win_ref, o_ref, winb_ref, *, n_valid_rows):
    c = c_ref[...]
    c_act = (c * jax.nn.sigmoid(c)).astype(BF16)
    o_ref[...] = jnp.dot(c_act, w_ref[...].astype(BF16), preferred_element_type=F32) + b_ref[...]
    slab = win_ref.shape[0]
    row = pl.program_id(0) * slab + lax.broadcasted_iota(jnp.int32, (slab, 1), 0)
    winb_ref[...] = jnp.where(row < n_valid_rows, win_ref[...], 0.0).astype(BF16)


def _ada_mod(c_pad, w_ada, b_ada, w_in_t, *, n_steps):
    mp, d = c_pad.shape
    n = w_ada.shape[1]
    tn = n // n_steps
    in_w, rows = w_in_t.shape
    in_w_pad = -(-in_w // V7X_LANES) * V7X_LANES
    slab = in_w_pad // n_steps
    assert n % n_steps == 0 and tn % V7X_LANES == 0 and in_w_pad % n_steps == 0 and slab % 16 == 0
    return pl.pallas_call(
        functools.partial(_ada_kernel, n_valid_rows=in_w),
        out_shape=(jax.ShapeDtypeStruct((mp, n), F32), jax.ShapeDtypeStruct((in_w_pad, rows), BF16)),
        grid=(n_steps,),
        in_specs=[pl.BlockSpec((mp, d), lambda j: (0, 0)),
                  pl.BlockSpec((d, tn), lambda j: (0, j)),
                  pl.BlockSpec((1, tn), lambda j: (0, j)),
                  pl.BlockSpec((slab, rows), lambda j: (j, 0))],
        out_specs=(pl.BlockSpec((mp, tn), lambda j: (0, j)),
                   pl.BlockSpec((slab, rows), lambda j: (j, 0))),
        compiler_params=_params(("arbitrary",)),
        name="ada_mod",
    )(c_pad, w_ada, b_ada, w_in_t)


def _in_proj_kernel(x_ref, g_ref, sc_ref, sh_ref, w_ref,
                    zp_ref, q_ref, k_ref, vt_ref, iq_ref, ik_ref, iwt_ref, *, splits):
    pool_w, attn_w, kv_w, iq_w = splits
    h = (_rms(x_ref[...], g_ref[...]) * (1.0 + sc_ref[0]) + sh_ref[0]).astype(BF16)

    def proj(lo, width):
        return lax.dot_general(h, w_ref[lo:lo + width, :], NT_DIMS, preferred_element_type=F32)

    off = 0
    zp_ref[...] = proj(off, pool_w).astype(BF16)
    off += pool_w
    zq = proj(off, attn_w)
    for hd in range(attn_w // HEAD_DIM):
        q_ref[hd] = (zq[:, hd * HEAD_DIM:(hd + 1) * HEAD_DIM] * SOFTMAX_EXP2_SCALE).astype(BF16)
    off += attn_w
    k_ref[...] = proj(off, kv_w).astype(BF16)
    off += kv_w
    zv_t = proj(off, kv_w).T
    off += kv_w
    ziq = proj(off, iq_w)
    for hd in range(iq_w // IDX_DIM):
        iq_ref[hd] = ziq[:, hd * IDX_DIM:(hd + 1) * IDX_DIM].astype(BF16)
    off += iq_w
    ik_ref[...] = proj(off, IDX_DIM).astype(BF16)
    off += IDX_DIM
    iwt_ref[...] = proj(off, w_ref.shape[0] - off).T[:IDX_HEADS]
    vt_blk = vt_ref.shape[2]
    ones = jnp.ones((ONES_ROWS, vt_blk), BF16)
    for cb in range(vt_ref.shape[0]):
        for g in range(KV_HEADS):
            base = g * (HEAD_DIM + ONES_ROWS)
            vt_ref[cb, base:base + HEAD_DIM, :] = zv_t[g * HEAD_DIM:(g + 1) * HEAD_DIM,
                                                       cb * vt_blk:(cb + 1) * vt_blk].astype(BF16)
            vt_ref[cb, base + HEAD_DIM:base + HEAD_DIM + ONES_ROWS, :] = ones


def _in_proj(x2, g, scale, shift, w_in, *, tm, seq, splits, vt_blk):
    m, d = x2.shape
    pool_w, attn_w, kv_w, iq_w = splits
    n_heads = attn_w // HEAD_DIM
    per_b = seq // tm
    assert tm % vt_blk == 0
    row = lambda i: (i, 0)
    col = lambda i: (0, i)
    hrow = lambda i: (0, i, 0)
    hrow0 = lambda i: (i, 0, 0)
    return pl.pallas_call(
        functools.partial(_in_proj_kernel, splits=splits),
        out_shape=(jax.ShapeDtypeStruct((m, pool_w), BF16),
                   jax.ShapeDtypeStruct((n_heads, m, HEAD_DIM), BF16),
                   jax.ShapeDtypeStruct((m, kv_w), BF16),
                   jax.ShapeDtypeStruct((m // vt_blk, VT_ROWS, vt_blk), BF16),
                   jax.ShapeDtypeStruct((IDX_HEADS, m, IDX_DIM), BF16),
                   jax.ShapeDtypeStruct((m, IDX_DIM), BF16),
                   jax.ShapeDtypeStruct((IDX_HEADS, m), F32)),
        grid=(m // tm,),
        in_specs=[pl.BlockSpec((tm, d), row),
                  pl.BlockSpec((1, d), lambda i: (0, 0)),
                  pl.BlockSpec((1, 1, d), lambda i: (i // per_b, 0, 0)),
                  pl.BlockSpec((1, 1, d), lambda i: (i // per_b, 0, 0)),
                  pl.BlockSpec(w_in.shape, lambda i: (0, 0), pipeline_mode=pl.Buffered(1))],
        out_specs=(pl.BlockSpec((tm, pool_w), row),
                   pl.BlockSpec((n_heads, tm, HEAD_DIM), hrow),
                   pl.BlockSpec((tm, kv_w), row),
                   pl.BlockSpec((tm // vt_blk, VT_ROWS, vt_blk), hrow0),
                   pl.BlockSpec((IDX_HEADS, tm, IDX_DIM), hrow),
                   pl.BlockSpec((tm, IDX_DIM), row),
                   pl.BlockSpec((IDX_HEADS, tm), col)),
        compiler_params=_params(("arbitrary",)),
        name="in_proj",
    )(x2, g, scale, shift, w_in)


def _pool_kernel(cur_ref, prev_ref, w_ref, ps_ref, g_ref, o_ref, ext_ref, mix_ref, *, halo):
    t = pl.program_id(1)
    tt = cur_ref.shape[0]
    ch = w_ref.shape[1]
    prev = prev_ref[...].astype(F32)
    ext_ref[0:halo, :] = jnp.where(t > 0, prev, jnp.zeros_like(prev))
    ext_ref[halo:halo + tt, :] = cur_ref[...].astype(F32)
    pos = t * tt + lax.broadcasted_iota(jnp.int32, (tt, 1), 0)
    for gi, win in enumerate(POOL_WINDOWS):
        cols = slice(gi * ch, (gi + 1) * ch)
        total = ext_ref[halo:halo + tt, cols]
        for j in range(1, win):
            total = total + ext_ref[halo - j:halo - j + tt, cols]
        inv_cnt = 1.0 / jnp.minimum(pos + 1, win).astype(F32)
        pooled = total * inv_cnt - ext_ref[halo:halo + tt, cols]
        mixed = jnp.dot(pooled.astype(BF16), w_ref[gi], preferred_element_type=F32)
        mix_ref[:, cols] = mixed * ps_ref[:, cols]
    o_ref[...] = _rms(mix_ref[...], g_ref[...]).astype(BF16)


def _pool_mixer(z_pool, w_pool, pool_scale, g_pool, *, batch, seq, tt):
    m, pw = z_pool.shape
    halo = max(POOL_WINDOWS)
    per_b = seq // tt
    hb = tt // halo
    return pl.pallas_call(
        functools.partial(_pool_kernel, halo=halo),
        out_shape=jax.ShapeDtypeStruct((m, pw), BF16),
        grid=(batch, per_b),
        in_specs=[pl.BlockSpec((tt, pw), lambda b, t: (b * per_b + t, 0)),
                  pl.BlockSpec((halo, pw), lambda b, t: (jnp.maximum((b * per_b + t) * hb - 1, 0), 0)),
                  pl.BlockSpec(w_pool.shape, lambda b, t: (0, 0, 0)),
                  pl.BlockSpec((1, pw), lambda b, t: (0, 0)),
                  pl.BlockSpec((1, pw), lambda b, t: (0, 0))],
        out_specs=pl.BlockSpec((tt, pw), lambda b, t: (b * per_b + t, 0)),
        scratch_shapes=[pltpu.VMEM((tt + halo, pw), F32), pltpu.VMEM((tt, pw), F32)],
        compiler_params=_params(("arbitrary", "arbitrary")),
        name="pool_mixer",
    )(z_pool, z_pool, w_pool, pool_scale, g_pool)


def _dsa_kernel(q_ref, iq_ref, iwt_ref, k_ref, vt_ref, ik_ref, g_ref, *rest, tq, tk, topk, n_heads, n_cast):
    cast_in, (o_ref,), cast_out = rest[:n_cast], rest[n_cast:n_cast + 1], rest[n_cast + 1:2 * n_cast + 1]
    sc_ref, bias_ref, acc_ref, m_ref, l_ref, s_ref = rest[2 * n_cast + 1:]
    for src_ref, dst_ref in zip(cast_in, cast_out):
        dst_ref[...] = src_ref[...].astype(BF16).reshape(dst_ref.shape)
    qi = pl.program_id(1)
    t0 = qi * tq
    nkv = lax.div(t0 + tq + tk - 1, tk)
    q_per_kv = n_heads // KV_HEADS
    idx_scale = (IDX_HEADS ** -0.5) * (IDX_DIM ** -0.5)
    q_pos = t0 + lax.broadcasted_iota(jnp.int32, (tk, tq), 1)
    k_off = lax.broadcasted_iota(jnp.int32, (tk, tq), 0)
    w_idx = iwt_ref[...] * idx_scale

    sub = sc_ref.shape[1]
    sub_per_blk = tk // sub
    n_sub = lax.div(t0 + tq + sub - 1, sub)

    def key_rows(kb):
        return pl.ds(pl.multiple_of(kb * tk, tk), tk)

    def score_block(kb, carry):
        lo, hi = carry
        ik = ik_ref[key_rows(kb), :]
        acc = jnp.zeros((tk, tq), F32)
        for h in range(IDX_HEADS):
            y = lax.dot_general(ik, iq_ref[h], NT_DIMS, preferred_element_type=F32)
            acc = acc + w_idx[h:h + 1, :] * jnp.maximum(y, 0.0)
        causal = (kb * tk + k_off) <= q_pos
        masked = jnp.where(causal, acc, -jnp.inf)
        for c in range(sub_per_blk):
            sc_ref[kb * sub_per_blk + c] = masked[c * sub:(c + 1) * sub]
        lo = jnp.minimum(lo, jnp.min(jnp.where(causal, acc, jnp.inf), axis=0, keepdims=True))
        hi = jnp.maximum(hi, jnp.max(jnp.where(causal, acc, -jnp.inf), axis=0, keepdims=True))
        return lo, hi

    lo0, hi0 = lax.fori_loop(0, nkv, score_block,
                             (jnp.full((1, tq), jnp.inf, F32), jnp.full((1, tq), -jnp.inf, F32)))

    def count_ge(cand):
        def body(sb, part):
            hit = jnp.where(sc_ref[sb] >= cand, 1.0, 0.0)
            return part + jnp.sum(hit.reshape(COUNT_CHAINS, sub // COUNT_CHAINS, tq), axis=0)
        part = lax.fori_loop(0, n_sub, body, jnp.zeros((sub // COUNT_CHAINS, tq), F32))
        return jnp.sum(part, axis=0, keepdims=True)

    few_keys = (t0 + lax.broadcasted_iota(jnp.int32, (1, tq), 1)) < topk

    def bisect_cond(state):
        it, _, _, done = state
        return (it < MAX_BISECT_STEPS) & (jnp.min(done) < 0.5)

    def bisect_once(lo, hi, done):
        mid = 0.5 * lo + 0.5 * hi
        cnt = count_ge(mid)
        ge = cnt >= float(topk)
        move = (done < 0.5) & (mid > lo) & (mid < hi)
        new_lo = jnp.where(move & ge, mid, lo)
        new_hi = jnp.where(move & ~ge, mid, hi)
        new_done = jnp.where(move & (cnt != float(topk)), 0.0, 1.0)
        return new_lo, new_hi, new_done

    def bisect_step(state):
        it, lo, hi, done = state
        for _ in range(BISECT_STEPS_PER_CHECK):
            lo, hi, done = bisect_once(lo, hi, done)
        return it + BISECT_STEPS_PER_CHECK, lo, hi, done

    _, thr, _, _ = lax.while_loop(
        bisect_cond, bisect_step,
        (jnp.int32(0), lo0, hi0, jnp.where(few_keys, 1.0, 0.0)))

    def bias_block(sb, carry):
        bias_ref[sb] = jnp.where(sc_ref[sb] >= thr, 0.0, -jnp.inf)
        return carry

    lax.fori_loop(0, nkv * sub_per_blk, bias_block, 0)

    gq = q_per_kv * tq
    m_ref[...] = jnp.full(m_ref.shape, -jnp.inf, F32)
    l_ref[...] = jnp.zeros(l_ref.shape, F32)
    acc_ref[...] = jnp.zeros(acc_ref.shape, F32)

    def attn_block(kb, carry):
        rows = key_rows(kb)
        bias = jnp.concatenate([bias_ref[kb * sub_per_blk + c] for c in range(sub_per_blk)], axis=0)
        bias_g = jnp.concatenate([bias] * q_per_kv, axis=1)
        for g in range(KV_HEADS):
            kblk = k_ref[rows, g * HEAD_DIM:(g + 1) * HEAD_DIM]
            q_g = q_ref[g * q_per_kv:(g + 1) * q_per_kv].reshape(gq, HEAD_DIM)
            s_ref[g] = lax.dot_general(kblk, q_g, NT_DIMS, preferred_element_type=F32)
        for g in range(KV_HEADS):
            v_aug = vt_ref[kb, g * (HEAD_DIM + ONES_ROWS):(g + 1) * (HEAD_DIM + ONES_ROWS), :]
            s = s_ref[g] + bias_g
            m_run = m_ref[g:g + 1, :]
            m_new = jnp.maximum(m_run, jnp.max(s, axis=0, keepdims=True))
            m_safe = jnp.where(m_new == -jnp.inf, 0.0, m_new)
            alpha = jnp.exp2(m_run - m_safe)
            p = jnp.exp2(s - m_safe).astype(BF16)
            pv = jnp.dot(v_aug, p, preferred_element_type=F32)
            arows = slice(g * HEAD_DIM, (g + 1) * HEAD_DIM)
            acc_ref[arows, :] = alpha * acc_ref[arows, :] + pv[:HEAD_DIM]
            l_ref[g:g + 1, :] = alpha * l_ref[g:g + 1, :] + pv[HEAD_DIM:HEAD_DIM + 1]
            m_ref[g:g + 1, :] = m_new
        return carry

    lax.fori_loop(0, nkv, attn_block, 0)

    def head_view(hd):
        g, j = divmod(hd, q_per_kv)
        return g, slice(g * HEAD_DIM, (g + 1) * HEAD_DIM), slice(j * tq, (j + 1) * tq)

    ssq = jnp.zeros((1, tq), F32)
    for hd in range(n_heads):
        g, arows, lanes = head_view(hd)
        o_h = acc_ref[arows, lanes] / l_ref[g:g + 1, lanes]
        acc_ref[arows, lanes] = o_h
        ssq = ssq + jnp.sum(o_h * o_h, axis=0, keepdims=True)
    inv_rms = lax.rsqrt(ssq / float(n_heads * HEAD_DIM) + RMS_EPS)
    for hd in range(n_heads):
        _, arows, lanes = head_view(hd)
        o_t = (acc_ref[arows, lanes] * inv_rms).T
        cols = slice(hd * HEAD_DIM, (hd + 1) * HEAD_DIM)
        o_ref[:, cols] = (o_t * g_ref[:, cols]).astype(BF16)


def _dsa_attention(q, iq, iwt, k, vt, ik, g_attn, cast_weights, *, batch, seq, tq, tk):
    n_heads, m, _ = q.shape
    attn_w = n_heads * HEAD_DIM
    kv_w = k.shape[1]
    per_b = seq // tq
    n_steps = batch * per_b
    topk = min(INDEX_TOPK, seq // 4)
    gq = (n_heads // KV_HEADS) * tq
    qmap = lambda b, i: (0, b * per_b + i, 0)
    cast_in_specs, cast_out_specs, cast_shapes = [], [], []
    for w, axis, col_tile in cast_weights:
        slab = w.shape[axis] // n_steps
        assert w.shape[axis] % n_steps == 0 and slab % (16 if axis == 0 else 128) == 0
        blk = list(w.shape)
        blk[axis] = slab
        imap = (lambda b, i: (b * per_b + i, 0)) if axis == 0 else (lambda b, i: (0, b * per_b + i))
        cast_in_specs.append(pl.BlockSpec(tuple(blk), imap))
        if col_tile is None:
            cast_out_specs.append(pl.BlockSpec(tuple(blk), imap))
            cast_shapes.append(jax.ShapeDtypeStruct(w.shape, BF16))
        else:
            assert axis == 1 and col_tile % slab == 0 and w.shape[1] % col_tile == 0
            r = col_tile // slab
            cast_out_specs.append(pl.BlockSpec(
                (1, w.shape[0], slab), lambda b, i, r=r: ((b * per_b + i) // r, 0, (b * per_b + i) % r)))
            cast_shapes.append(jax.ShapeDtypeStruct((w.shape[1] // col_tile, w.shape[0], col_tile), BF16))
    n_cast = len(cast_weights)
    outs = pl.pallas_call(
        functools.partial(_dsa_kernel, tq=tq, tk=tk, topk=topk, n_heads=n_heads, n_cast=n_cast),
        out_shape=[jax.ShapeDtypeStruct((m, attn_w), BF16)] + cast_shapes,
        grid=(batch, per_b),
        in_specs=[pl.BlockSpec((n_heads, tq, HEAD_DIM), qmap),
                  pl.BlockSpec((IDX_HEADS, tq, IDX_DIM), qmap),
                  pl.BlockSpec((IDX_HEADS, tq), lambda b, i: (0, b * per_b + i)),
                  pl.BlockSpec((seq, kv_w), lambda b, i: (b, 0)),
                  pl.BlockSpec((seq // tk, VT_ROWS, tk), lambda b, i: (b, 0, 0)),
                  pl.BlockSpec((seq, IDX_DIM), lambda b, i: (b, 0)),
                  pl.BlockSpec((1, attn_w), lambda b, i: (0, 0))] + cast_in_specs,
        out_specs=[pl.BlockSpec((tq, attn_w), lambda b, i: (b * per_b + i, 0))] + cast_out_specs,
        scratch_shapes=[pltpu.VMEM((seq // DSA_SUB, DSA_SUB, tq), F32),
                        pltpu.VMEM((seq // DSA_SUB, DSA_SUB, tq), F32),
                        pltpu.VMEM((KV_HEADS * HEAD_DIM, gq), F32),
                        pltpu.VMEM((8, gq), F32),
                        pltpu.VMEM((8, gq), F32),
                        pltpu.VMEM((KV_HEADS, tk, gq), F32)],
        compiler_params=_params(("arbitrary", "arbitrary")),
        name="dsa_attn",
    )(q, iq, iwt, k, vt, ik, g_attn, *[w for w, _, _ in cast_weights])
    return outs[0], outs[1:]


def _out_proj_kernel(pool_ref, attn_ref, w_ref, x_ref, gpost_ref, gate_ref, gpre_ref, sc_ref, sh_ref,
                     x1_ref, h2_ref):
    pw = pool_ref.shape[1]
    mix = jnp.dot(pool_ref[...], w_ref[0:pw, :], preferred_element_type=F32)
    mix = mix + jnp.dot(attn_ref[...], w_ref[pw:, :], preferred_element_type=F32)
    x1 = x_ref[...] + gate_ref[0] * _rms(mix, gpost_ref[...])
    x1_ref[...] = x1
    h2_ref[...] = (_rms(x1, gpre_ref[...]) * (1.0 + sc_ref[0]) + sh_ref[0]).astype(BF16)


def _out_proj(pool_n, attn_n, w_out, x2, g_post, gate, g_pre, scale, shift, *, tm, seq):
    m, d = x2.shape
    pw, aw = pool_n.shape[1], attn_n.shape[1]
    per_b = seq // tm
    row = lambda i: (i, 0)
    vec = lambda i: (0, 0)
    bvec = lambda i: (i // per_b, 0, 0)
    return pl.pallas_call(
        _out_proj_kernel,
        out_shape=(jax.ShapeDtypeStruct((m, d), F32), jax.ShapeDtypeStruct((m, d), BF16)),
        grid=(m // tm,),
        in_specs=[pl.BlockSpec((tm, pw), row),
                  pl.BlockSpec((tm, aw), row),
                  pl.BlockSpec(w_out.shape, vec, pipeline_mode=pl.Buffered(1)),
                  pl.BlockSpec((tm, d), row),
                  pl.BlockSpec((1, d), vec),
                  pl.BlockSpec((1, 1, d), bvec),
                  pl.BlockSpec((1, d), vec),
                  pl.BlockSpec((1, 1, d), bvec),
                  pl.BlockSpec((1, 1, d), bvec)],
        out_specs=(pl.BlockSpec((tm, d), row), pl.BlockSpec((tm, d), row)),
        compiler_params=_params(("arbitrary",)),
        name="out_proj",
    )(pool_n, attn_n, w_out, x2, g_post, gate, g_pre, scale, shift)


def _ffn_kernel(h_ref, w1_ref, w2_ref, x_ref, g_ref, gate_ref, o_ref, acc_ref):
    j = pl.program_id(1)

    @pl.when(j == 0)
    def _():
        acc_ref[...] = jnp.zeros_like(acc_ref)

    a = jnp.dot(h_ref[...], w1_ref[0], preferred_element_type=F32)
    a = jnp.square(jnp.maximum(a, 0.0)).astype(BF16)
    acc_ref[...] += jnp.dot(a, w2_ref[...], preferred_element_type=F32)

    @pl.when(j == pl.num_programs(1) - 1)
    def _():
        o_ref[...] = x_ref[...] + gate_ref[0] * _rms(acc_ref[...], g_ref[...])


def _ffn(h2, w1, w2, x1, g_post, gate, *, tm, seq):
    m, d = x1.shape
    n_f, _, tf = w1.shape
    per_b = seq // tm
    return pl.pallas_call(
        _ffn_kernel,
        out_shape=jax.ShapeDtypeStruct((m, d), F32),
        grid=(m // tm, n_f),
        in_specs=[pl.BlockSpec((tm, d), lambda i, j: (i, 0)),
                  pl.BlockSpec((1, d, tf), lambda i, j: (j, 0, 0)),
                  pl.BlockSpec((tf, d), lambda i, j: (j, 0)),
                  pl.BlockSpec((tm, d), lambda i, j: (i, 0)),
                  pl.BlockSpec((1, d), lambda i, j: (0, 0)),
                  pl.BlockSpec((1, 1, d), lambda i, j: (i // per_b, 0, 0))],
        out_specs=pl.BlockSpec((tm, d), lambda i, j: (i, 0)),
        scratch_shapes=[pltpu.VMEM((tm, d), F32)],
        compiler_params=_params(("arbitrary", "arbitrary")),
        name="ffn",
    )(h2, w1, w2, x1, g_post, gate)


def kernel(x, c, w_ada, b_ada, g_pre_mix, g_post_mix, g_pre_ffn, g_post_ffn, w_in, w_pool, pool_scale,
           g_pool_out, g_attn_out, w_out, w_ff1, w_ff2):
    batch, seq, d = x.shape
    depth = w_ada.shape[0]
    m = batch * seq
    pool_w = pool_scale.shape[1]
    attn_w = g_attn_out.shape[1]
    kv_w = KV_HEADS * HEAD_DIM
    iq_w = IDX_HEADS * IDX_DIM
    splits = (pool_w, attn_w, kv_w, iq_w)
    in_w = w_in.shape[2]
    assert in_w == pool_w + attn_w + 2 * kv_w + iq_w + IDX_DIM + IDX_HEADS

    x2 = x.reshape(m, d)
    c_pad = jnp.zeros((8, d), F32).at[:batch].set(c)
    for l in range(depth):
        mod, w_in_b = _ada_mod(c_pad, w_ada[l], b_ada[l][None, :], w_in[l].T, n_steps=16)
        mod = mod[:batch]
        shift1, scale1, gate1, shift2, scale2, gate2 = [
            mod[:, i * d:(i + 1) * d][:, None, :] for i in range(6)]

        z_pool, q, k, vt, iq, ik, iwt = _in_proj(
            x2, g_pre_mix[l][None, :], scale1, shift1, w_in_b, tm=512, seq=seq, splits=splits,
            vt_blk=DSA_TK)

        pool_n = _pool_mixer(z_pool, w_pool[l].astype(BF16), pool_scale[l][None, :], g_pool_out[l][None, :],
                             batch=batch, seq=seq, tt=512)
        attn_n, (w_out_b, w_ff1_b, w_ff2_b) = _dsa_attention(
            q, iq, iwt, k, vt, ik, g_attn_out[l][None, :],
            [(w_out[l], 0, None), (w_ff1[l], 1, FFN_TF), (w_ff2[l], 0, None)],
            batch=batch, seq=seq, tq=DSA_TQ, tk=DSA_TK)

        x1, h2 = _out_proj(pool_n, attn_n, w_out_b, x2, g_post_mix[l][None, :], gate1,
                           g_pre_ffn[l][None, :], scale2, shift2, tm=512, seq=seq)
        x2 = _ffn(h2, w_ff1_b, w_ff2_b, x1, g_post_ffn[l][None, :], gate2, tm=512, seq=seq)
    return x2.reshape(batch, seq, d)
```
